```python
import jax, jax.numpy as jnp
from jax import lax
import numpy as np

D_MODEL = 2048
BATCH = 2
SEQ = 16384
DEPTH = 2

HEAD_DIM = 128
GDN_HEADS = 8
HGRN_HEADS = 8
GDN_WIDTH = GDN_HEADS * HEAD_DIM
HGRN_WIDTH = HGRN_HEADS * HEAD_DIM
MIX_WIDTH = GDN_WIDTH + HGRN_WIDTH
N_DIR = 2
CONV_K = 7
CHUNK = 64
N_EXPERTS = 16
EXPERT_FF = 2048
CAPACITY_FACTOR = 2
NORM_EPS = 1e-6
IN_SPLITS = (GDN_WIDTH, GDN_WIDTH, GDN_WIDTH, GDN_WIDTH, N_DIR * GDN_HEADS, N_DIR * GDN_HEADS,
             HGRN_WIDTH, N_DIR * HGRN_WIDTH, HGRN_WIDTH, HGRN_WIDTH)
N_IN = sum(IN_SPLITS)

kernel_name = "hybrid_gdn_hgrn2_ecmoe_encoder"


def rms_norm(x, w):
    xf = x.astype(jnp.float32)
    y = xf * lax.rsqrt(jnp.mean(xf * xf, axis=-1, keepdims=True) + NORM_EPS)
    return (y * w.astype(jnp.float32)).astype(x.dtype)


def gated_rms_norm(o, z, w):
    of = o.astype(jnp.float32)
    y = of * lax.rsqrt(jnp.mean(of * of, axis=-1, keepdims=True) + NORM_EPS)
    return y * w.astype(jnp.float32) * jax.nn.silu(z.astype(jnp.float32))


def l2_normalize(x):
    return x * lax.rsqrt(jnp.sum(x * x, axis=-1, keepdims=True) + NORM_EPS)


def centred_depthwise_conv(x, w):
    c = x.shape[-1]
    return lax.conv_general_dilated(
        x, w[:, None, :].astype(x.dtype), window_strides=(1,),
        padding=[((CONV_K - 1) // 2, CONV_K // 2)],
        dimension_numbers=("NWC", "WIO", "NWC"), feature_group_count=c)


def flip_seq(a):
    return jnp.flip(a, axis=2)


def masked_decay(diff, mask):
    return jnp.where(mask, jnp.exp(jnp.minimum(diff, 0.0)), 0.0)


def gated_delta_chunk_scan(q, k, v, g, beta):
    b, h, t, dk = q.shape
    dv = v.shape[-1]
    n = t // CHUNK
    rs = lambda a: a.reshape(b, h, n, CHUNK, *a.shape[3:])
    q, k, v, g, beta = rs(q), rs(k), rs(v), rs(g), rs(beta)
    gc = jnp.cumsum(g, axis=-1)
    incl = jnp.tril(jnp.ones((CHUNK, CHUNK), dtype=bool))
    strict = jnp.tril(jnp.ones((CHUNK, CHUNK), dtype=bool), k=-1)
    decay = masked_decay(gc[..., :, None] - gc[..., None, :], incl)
    kb = k * beta[..., None]
    a_mat = jnp.where(strict, jnp.einsum("bhntk,bhnsk->bhnts", kb, k) * decay, 0.0)
    lmat = a_mat + jnp.eye(CHUNK, dtype=q.dtype)
    u = lax.linalg.triangular_solve(lmat, v * beta[..., None], left_side=True, lower=True)
    w = lax.linalg.triangular_solve(lmat, kb * jnp.exp(gc)[..., None], left_side=True, lower=True)
    qk = jnp.einsum("bhntk,bhnsk->bhnts", q, k) * decay
    qg = q * jnp.exp(gc)[..., None]
    kg = k * jnp.exp(gc[..., -1:] - gc)[..., None]
    glast = jnp.exp(gc[..., -1])

    def step(state, xs):
        u_n, w_n, qk_n, qg_n, kg_n, gl_n = xs
        v_new = u_n - jnp.einsum("bhtk,bhkv->bhtv", w_n, state)
        o = jnp.einsum("bhtk,bhkv->bhtv", qg_n, state) + jnp.einsum("bhts,bhsv->bhtv", qk_n, v_new)
        state = gl_n[..., None, None] * state + jnp.einsum("bhsk,bhsv->bhkv", kg_n, v_new)
        return state, o

    xs = tuple(jnp.moveaxis(a, 2, 0) for a in (u, w, qk, qg, kg, glast))
    s0 = jnp.zeros((b, h, dk, dv), q.dtype)
    _, o = lax.scan(step, s0, xs)
    return jnp.moveaxis(o, 0, 2).reshape(b, h, t, dv)


def hgrn2_chunk_scan(q, k, v, logf):
    b, h, t, dk = q.shape
    dv = v.shape[-1]
    n = t // CHUNK
    rs = lambda a: jnp.moveaxis(a.reshape(b, h, n, CHUNK, a.shape[-1]), 2, 0)
    incl = jnp.tril(jnp.ones((CHUNK, CHUNK), dtype=bool))[:, :, None]

    def step(state, xs):
        q_n, k_n, v_n, lf_n = xs
        bcum = jnp.cumsum(lf_n, axis=-2)
        decay = masked_decay(bcum[:, :, :, None, :] - bcum[:, :, None, :, :], incl)
        attn = jnp.einsum("bhtsk,bhsk->bhts", decay * q_n[:, :, :, None, :], k_n)
        o = jnp.einsum("bhtk,bhkv->bhtv", q_n * jnp.exp(bcum), state) + jnp.einsum("bhts,bhsv->bhtv", attn, v_n)
        blast = bcum[:, :, -1:, :]
        state = jnp.exp(blast[:, :, 0, :])[..., None] * state + jnp.einsum(
            "bhsk,bhsv->bhkv", k_n * jnp.exp(jnp.minimum(blast - bcum, 0.0)), v_n)
        return state, o

    s0 = jnp.zeros((b, h, dk, dv), q.dtype)
    _, o = lax.scan(step, s0, (rs(q), rs(k), rs(v), rs(logf)))
    return jnp.moveaxis(o, 0, 2).reshape(b, h, t, dv)


def token_mixer(u, w_in, conv_w, a_log, dt_bias, gdn_norm_w, lower_bound, hgrn_norm_w, w_out):
    b, s, _ = u.shape
    f32 = jnp.float32
    proj = jnp.einsum("bsd,dn->bsn", u, w_in)
    aq, ak, av, az, abeta, aalpha, bq, bfg, bi, bg = jnp.split(proj, np.cumsum(IN_SPLITS)[:-1], axis=-1)
    to_heads = lambda a, nh: a.reshape(b, s, nh, HEAD_DIM).astype(f32).transpose(0, 2, 1, 3)

    qkv = jax.nn.silu(centred_depthwise_conv(jnp.concatenate([aq, ak, av], axis=-1), conv_w))
    cq, ck, cv = jnp.split(qkv, 3, axis=-1)
    q_a = l2_normalize(to_heads(cq, GDN_HEADS)) * (HEAD_DIM ** -0.5)
    k_a = l2_normalize(to_heads(ck, GDN_HEADS))
    v_a = to_heads(cv, GDN_HEADS)
    beta = jax.nn.sigmoid(abeta.reshape(b, s, N_DIR, GDN_HEADS).astype(f32)).transpose(2, 0, 3, 1)
    g = (-jnp.exp(a_log.astype(f32)) * jax.nn.softplus(
        aalpha.reshape(b, s, N_DIR, GDN_HEADS).astype(f32) + dt_bias.astype(f32))).transpose(2, 0, 3, 1)
    o_a = gated_delta_chunk_scan(q_a, k_a, v_a, g[0], beta[0]) + flip_seq(gated_delta_chunk_scan(
        flip_seq(q_a), flip_seq(k_a), flip_seq(v_a), flip_seq(g[1]), flip_seq(beta[1])))
    o_a = gated_rms_norm(o_a.transpose(0, 2, 1, 3), az.reshape(b, s, GDN_HEADS, HEAD_DIM), gdn_norm_w)

    lb = lower_bound.astype(f32)
    fpre = bfg.reshape(b, s, N_DIR, HGRN_WIDTH).astype(f32)
    f_gate = lb + (1.0 - lb) * jax.nn.sigmoid(fpre)
    logf = jnp.log(jnp.maximum(f_gate, jnp.finfo(f32).tiny))
    kin = (1.0 - lb) * jax.nn.sigmoid(-fpre)
    dir_heads = lambda a: a.reshape(b, s, N_DIR, HGRN_HEADS, HEAD_DIM).transpose(2, 0, 3, 1, 4)
    logf, kin = dir_heads(logf), dir_heads(kin)
    q_b = to_heads(bq, HGRN_HEADS)
    v_b = to_heads(bi, HGRN_HEADS)
    o_b = hgrn2_chunk_scan(q_b, kin[0], v_b, logf[0]) + flip_seq(hgrn2_chunk_scan(
        flip_seq(q_b), flip_seq(kin[1]), flip_seq(v_b), flip_seq(logf[1])))
    o_b = gated_rms_norm(o_b.transpose(0, 2, 1, 3), bg.reshape(b, s, HGRN_HEADS, HEAD_DIM), hgrn_norm_w)

    mixed = jnp.concatenate([o_a.reshape(b, s, GDN_WIDTH), o_b.reshape(b, s, HGRN_WIDTH)], axis=-1)
    return jnp.einsum("bsm,md->bsd", mixed.astype(u.dtype), w_out)


def expert_choice_moe(u, w_router, w_gate, w_up, w_down):
    b, s, d = u.shape
    cap = CAPACITY_FACTOR * s // N_EXPERTS
    probs = jax.nn.softmax(jnp.einsum("bsd,de->bse", u, w_router).astype(jnp.float32), axis=-1)
    gates, idx = lax.top_k(jnp.swapaxes(probs, 1, 2), cap)
    xs = jax.vmap(lambda ub, ib: ub[ib])(u, idx)
    hid = jax.nn.silu(jnp.einsum("becd,edf->becf", xs, w_gate)) * jnp.einsum("becd,edf->becf", xs, w_up)
    y = jnp.einsum("becf,efd->becd", hid, w_down) * gates[..., None].astype(u.dtype)
    return jax.vmap(lambda yb, ib: jnp.zeros((s, d), yb.dtype).at[ib.reshape(-1)].add(yb.reshape(-1, d)))(y, idx)


def setup_inputs(seed: int = 0) -> dict:
    key = jax.random.key(seed)
    ks = jax.random.split(key, 20)
    f32 = jnp.float32
    nrm = lambda k, shape, scale: jax.random.normal(k, shape, f32) * scale
    dt = jnp.exp(jax.random.uniform(ks[6], (DEPTH, N_DIR, GDN_HEADS), f32) * (np.log(0.1) - np.log(0.001)) + np.log(0.001))
    return {
        "x": jax.random.normal(ks[0], (BATCH, SEQ, D_MODEL), f32),
        "norm_mix": 1.0 + nrm(ks[1], (DEPTH, D_MODEL), 0.02),
        "norm_ffn": 1.0 + nrm(ks[2], (DEPTH, D_MODEL), 0.02),
        "norm_final": 1.0 + nrm(ks[3], (D_MODEL,), 0.02),
        "w_in": nrm(ks[4], (DEPTH, D_MODEL, N_IN), D_MODEL ** -0.5),
        "conv_w": nrm(ks[5], (DEPTH, CONV_K, 3 * GDN_WIDTH), CONV_K ** -0.5),
        "gdn_a_log": jnp.log(jax.random.uniform(ks[7], (DEPTH, N_DIR, GDN_HEADS), f32, 1.0, 16.0)),
        "gdn_dt_bias": dt + jnp.log(-jnp.expm1(-dt)),
        "gdn_norm": 1.0 + nrm(ks[8], (DEPTH, HEAD_DIM), 0.02),
        "hgrn_lower_bounds": nrm(ks[9], (DEPTH, N_DIR, HGRN_WIDTH), 0.5),
        "hgrn_norm": 1.0 + nrm(ks[10], (DEPTH, HEAD_DIM), 0.02),
        "w_out": nrm(ks[11], (DEPTH, MIX_WIDTH, D_MODEL), MIX_WIDTH ** -0.5),
        "w_router": nrm(ks[12], (DEPTH, D_MODEL, N_EXPERTS), D_MODEL ** -0.5),
        "w_gate": nrm(ks[13], (DEPTH, N_EXPERTS, D_MODEL, EXPERT_FF), D_MODEL ** -0.5),
        "w_up": nrm(ks[14], (DEPTH, N_EXPERTS, D_MODEL, EXPERT_FF), D_MODEL ** -0.5),
        "w_down": nrm(ks[15], (DEPTH, N_EXPERTS, EXPERT_FF, D_MODEL), EXPERT_FF ** -0.5),
    }


def reference(x, norm_mix, norm_ffn, norm_final, w_in, conv_w, gdn_a_log, gdn_dt_bias, gdn_norm,
              hgrn_lower_bounds, hgrn_norm, w_out, w_router, w_gate, w_up, w_down):
    p = jax.nn.softmax(hgrn_lower_bounds.astype(jnp.float32), axis=0)
    lower_bound = jnp.cumsum(p, axis=0) - p[0]
    h = x
    for l in range(DEPTH):
        u = rms_norm(h, norm_mix[l])
        h = h + token_mixer(u, w_in[l], conv_w[l], gdn_a_log[l], gdn_dt_bias[l], gdn_norm[l],
                            lower_bound[l], hgrn_norm[l], w_out[l])
        u = rms_norm(h, norm_ffn[l])
        h = h + expert_choice_moe(u, w_router[l], w_gate[l], w_up[l], w_down[l])
    return rms_norm(h, norm_final)
```

```python
import functools

import jax
import jax.numpy as jnp
from jax import lax
from jax.experimental import pallas as pl
from jax.experimental.pallas import tpu as pltpu

F32, BF16, I32 = jnp.float32, jnp.bfloat16, jnp.int32
NORM_EPS = 1e-6
CHUNK = 64
CAPACITY_FACTOR = 2
LANES = 128
SUBLANES = 8
BF16_ROWS = 16
SMEM_BLOCK = 1024
VMEM_LIMIT = 56 * 1024 * 1024
NT_DIMS = (((1,), (1,)), ((), ()))
TN_DIMS = (((0,), (0,)), ((), ()))


def _tile(n, pref):
    return pref if n % pref == 0 else n


def _params(n_axes):
    return pltpu.CompilerParams(dimension_semantics=("arbitrary",) * n_axes,
                                vmem_limit_bytes=VMEM_LIMIT)


def _dot(a, b):
    return jnp.dot(a, b, preferred_element_type=F32)


def _dot_nt(a, b):
    return lax.dot_general(a, b, NT_DIMS, preferred_element_type=F32)


def _dot_tn(a, b):
    return lax.dot_general(a, b, TN_DIMS, preferred_element_type=F32)


def _rms(x, w):
    return x * lax.rsqrt(jnp.mean(x * x, axis=-1, keepdims=True) + NORM_EPS) * w


def _softplus(x):
    return jnp.maximum(x, 0.0) + jnp.log(1.0 + jnp.exp(-jnp.abs(x)))


def _cumsum_rows(x, rev):
    n = x.shape[0]
    row = lax.broadcasted_iota(I32, x.shape, 0)
    k = 1
    while k < n:
        if rev:
            x = x + jnp.where(row < n - k, pltpu.roll(x, n - k, 0), 0.0)
        else:
            x = x + jnp.where(row >= k, pltpu.roll(x, k, 0), 0.0)
        k *= 2
    return x


def _inproj_kernel(x_ref, nw_ref, w_ref, ws_ref, o_ref, os_ref, u_scr):
    @pl.when(pl.program_id(1) == 0)
    def _():
        ub = _rms(x_ref[...], nw_ref[...]).astype(BF16)
        u_scr[...] = ub
        os_ref[...] = _dot(ub, ws_ref[...])

    o_ref[...] = _dot(u_scr[...], w_ref[...])


def _in_projection(h, nw, w_main, w_small):
    t, d = h.shape
    n = w_main.shape[1]
    tm, tn = _tile(t, 512), _tile(n, 512)
    return pl.pallas_call(
        _inproj_kernel,
        grid=(t // tm, n // tn),
        in_specs=[pl.BlockSpec((tm, d), lambda i, j: (i, 0)),
                  pl.BlockSpec((1, d), lambda i, j: (0, 0)),
                  pl.BlockSpec((d, tn), lambda i, j: (0, j)),
                  pl.BlockSpec((d, LANES), lambda i, j: (0, 0))],
        out_specs=[pl.BlockSpec((tm, tn), lambda i, j: (i, j)),
                   pl.BlockSpec((tm, LANES), lambda i, j: (i, 0))],
        out_shape=[jax.ShapeDtypeStruct((t, n), F32), jax.ShapeDtypeStruct((t, LANES), F32)],
        scratch_shapes=[pltpu.VMEM((tm, d), BF16)],
        compiler_params=_params(2),
        name="in_projection",
    )(h, nw, w_main, w_small)


def _conv_kernel(prev_ref, x_ref, next_ref, w_ref, o_ref, ext_scr, *, rows, nblk, taps, hd):
    i = pl.program_id(1)
    g = pl.program_id(2)
    pad = SUBLANES
    ext_scr[0:pad, :] = jnp.where(i > 0, prev_ref[...], 0.0)
    ext_scr[pad:pad + rows, :] = x_ref[...]
    ext_scr[pad + rows:, :] = jnp.where(i < nblk - 1, next_ref[...], 0.0)
    first = pad - (taps - 1) // 2
    acc = ext_scr[pl.ds(first, rows), :] * w_ref[0:1, :]
    for j in range(1, taps):
        acc = acc + ext_scr[pl.ds(first + j, rows), :] * w_ref[j:j + 1, :]
    y = acc * jax.nn.sigmoid(acc)
    qscale = jnp.where(g == 0, hd ** -0.5, 1.0)
    for hh in range(y.shape[1] // hd):
        sl = slice(hh * hd, (hh + 1) * hd)
        yh = y[:, sl]
        r = lax.rsqrt(jnp.sum(yh * yh, axis=-1, keepdims=True) + NORM_EPS) * qscale
        o_ref[:, sl] = jnp.where(g == 2, yh, yh * r)


def _gdn_conv(proj3, conv_w, gw, hd):
    b, s, _ = proj3.shape
    taps = conv_w.shape[0]
    rows = _tile(s, 256)
    nblk = s // rows
    rb = rows // SUBLANES
    wpad = jnp.zeros((SUBLANES, 3 * gw), F32).at[:taps].set(conv_w)
    kern = functools.partial(_conv_kernel, rows=rows, nblk=nblk, taps=taps, hd=hd)
    return pl.pallas_call(
        kern,
        grid=(b, nblk, 3),
        in_specs=[pl.BlockSpec((None, SUBLANES, gw), lambda bb, i, g: (bb, jnp.maximum(i * rb - 1, 0), g)),
                  pl.BlockSpec((None, rows, gw), lambda bb, i, g: (bb, i, g)),
                  pl.BlockSpec((None, SUBLANES, gw),
                               lambda bb, i, g: (bb, jnp.minimum((i + 1) * rb, s // SUBLANES - 1), g)),
                  pl.BlockSpec((SUBLANES, gw), lambda bb, i, g: (0, g))],
        out_specs=pl.BlockSpec((None, rows, gw), lambda bb, i, g: (bb, i, g)),
        out_shape=jax.ShapeDtypeStruct((b, s, 3 * gw), F32),
        scratch_shapes=[pltpu.VMEM((rows + 2 * SUBLANES, gw), F32)],
        compiler_params=_params(3),
        name="gdn_conv",
    )(proj3, proj3, proj3, wpad)


def _unit_tri_inverse(a, eye):
    y = -a
    p = eye + y
    span = 2
    while span < a.shape[0]:
        yb = y.astype(BF16)
        y = _dot(yb, yb)
        p = p + _dot(p.astype(BF16), y.astype(BF16))
        span *= 2
    return p


def _gdn_kernel(q_ref, k_ref, v_ref, gs_ref, alog_ref, dt_ref, o_ref, s_scr, *, heads, hd, rev, d):
    c = CHUNK

    @pl.when(pl.program_id(1) == 0)
    def _():
        s_scr[...] = jnp.zeros_like(s_scr)

    gs = gs_ref[...]
    bt = jax.nn.sigmoid(gs)
    gall = -jnp.exp(alog_ref[...]) * _softplus(gs + dt_ref[...])
    gc = _cumsum_rows(gall, rev)
    gct = gc.T
    ti = lax.broadcasted_iota(I32, (c, c), 0)
    si = lax.broadcasted_iota(I32, (c, c), 1)
    incl = (ti <= si) if rev else (ti >= si)
    strict = (ti < si) if rev else (ti > si)
    eye = (ti == si).astype(F32)
    last = 0 if rev else c - 1
    for h in range(heads):
        a = 2 * heads + d * heads + h
        bcol = d * heads + h
        gcc = gc[:, a:a + 1]
        gcr = gct[a:a + 1, :]
        beta = bt[:, bcol:bcol + 1]
        gl = gc[last:last + 1, a:a + 1]
        sl = slice(h * hd, (h + 1) * hd)
        q = q_ref[:, sl]
        k = k_ref[:, sl]
        v = v_ref[:, sl]
        decay = jnp.where(incl, jnp.exp(jnp.minimum(gcc - gcr, 0.0)), 0.0)
        eg = jnp.exp(gcc)
        kb = k * beta
        kbf = k.astype(BF16)
        kk = _dot_nt(kb.astype(BF16), kbf)
        qk = _dot_nt(q.astype(BF16), kbf)
        tinv = _unit_tri_inverse(jnp.where(strict, kk * decay, 0.0), eye)
        rhs = jnp.concatenate([v * beta, kb * eg], axis=1).astype(BF16)
        uw = _dot(tinv.astype(BF16), rhs)
        u = uw[:, :hd]
        w = uw[:, hd:]
        qg = q * eg
        kg = k * jnp.exp(gl - gcc)
        state = s_scr[h]
        ws = _dot(jnp.concatenate([w, qg], axis=0).astype(BF16), state.astype(BF16))
        vnb = (u - ws[:c]).astype(BF16)
        o_ref[:, sl] = ws[c:] + _dot((qk * decay).astype(BF16), vnb)
        s_scr[h] = jnp.exp(gl) * state + _dot_tn(kg.astype(BF16), vnb)


def _gdn_scan(qkv, gs3, alog_row, dt_row, heads, hd, d):
    b, s, _ = qkv.shape
    gw = heads * hd
    nc = s // CHUNK
    rev = d == 1
    cidx = (lambda cc: nc - 1 - cc) if rev else (lambda cc: cc)
    kern = functools.partial(_gdn_kernel, heads=heads, hd=hd, rev=rev, d=d)
    col = lambda g: pl.BlockSpec((None, CHUNK, gw), lambda bb, cc: (bb, cidx(cc), g))
    row = pl.BlockSpec((1, LANES), lambda bb, cc: (0, 0))
    return pl.pallas_call(
        kern,
        grid=(b, nc),
        in_specs=[col(0), col(1), col(2),
                  pl.BlockSpec((None, CHUNK, LANES), lambda bb, cc: (bb, cidx(cc), 0)), row, row],
        out_specs=pl.BlockSpec((None, CHUNK, gw), lambda bb, cc: (bb, cidx(cc), 0)),
        out_shape=jax.ShapeDtypeStruct((b, s, gw), F32),
        scratch_shapes=[pltpu.VMEM((heads, hd, hd), F32)],
        compiler_params=_params(2),
        name="gdn_scan_bwd" if rev else "gdn_scan_fwd",
    )(qkv, qkv, qkv, gs3, alog_row, dt_row)


def _hgrn_kernel(q_ref, f_ref, v_ref, lbraw_ref, o_ref, s_scr, *, heads, hd, rev, layer):
    c = CHUNK

    @pl.when(pl.program_id(1) == 0)
    def _():
        s_scr[...] = jnp.zeros_like(s_scr)

    ti = lax.broadcasted_iota(I32, (c, c), 0)
    si = lax.broadcasted_iota(I32, (c, c), 1)
    row = lax.broadcasted_iota(I32, (c, hd), 0)
    diag = ti == si
    raw = lbraw_ref[...]
    ex = jnp.exp(raw - jnp.max(raw, axis=0, keepdims=True))
    share = ex / jnp.sum(ex, axis=0, keepdims=True)
    lb_all = jnp.zeros((1, share.shape[1]), F32)
    for j in range(1, layer + 1):
        lb_all = lb_all + share[j:j + 1, :]
    tiny = jnp.finfo(F32).tiny
    for h in range(heads):
        sl = slice(h * hd, (h + 1) * hd)
        q = q_ref[:, sl]
        fpre = f_ref[:, sl]
        vb = v_ref[:, sl].astype(BF16)
        lb = lb_all[:, sl]
        f = jnp.maximum(lb + (1.0 - lb) * jax.nn.sigmoid(fpre), tiny)
        k = (1.0 - lb) * jax.nn.sigmoid(-fpre)
        ip, es, tot = f, jnp.ones_like(f), f
        attn = jnp.where(diag, _dot_nt(q.astype(BF16), k.astype(BF16)), 0.0)
        m, shift = 1, 0
        while m < c:
            tb = ti >> shift
            sb = si >> shift
            if rev:
                pair = ((tb & 1) == 0) & (sb == tb + 1)
            else:
                pair = ((tb & 1) == 1) & (sb == tb - 1)
            attn = attn + jnp.where(pair, _dot_nt((q * ip).astype(BF16), (k * es).astype(BF16)), 0.0)
            upper = ((row >> shift) & 1) == 1
            below = pltpu.roll(tot, m, 0)
            above = pltpu.roll(tot, c - m, 0)
            if rev:
                ip = ip * jnp.where(upper, 1.0, above)
                es = es * jnp.where(upper, below, 1.0)
            else:
                ip = ip * jnp.where(upper, below, 1.0)
                es = es * jnp.where(upper, 1.0, above)
            tot = tot * jnp.where(upper, below, above)
            m, shift = 2 * m, shift + 1
        state_t = s_scr[h]
        o_ref[:, sl] = _dot_nt((q * ip).astype(BF16), state_t.astype(BF16)) + _dot(attn.astype(BF16), vb)
        s_scr[h] = tot[0:1, :] * state_t + _dot_tn(vb, (k * es).astype(BF16))


def _hgrn_scan(proj3, lbraw, heads, hd, d, layer, qcol, fcol, vcol):
    b, s, _ = proj3.shape
    hw = heads * hd
    nc = s // CHUNK
    rev = d == 1
    depth = lbraw.shape[1]
    cidx = (lambda cc: nc - 1 - cc) if rev else (lambda cc: cc)
    kern = functools.partial(_hgrn_kernel, heads=heads, hd=hd, rev=rev, layer=layer)
    col = lambda g: pl.BlockSpec((None, CHUNK, hw), lambda bb, cc: (bb, cidx(cc), g))
    return pl.pallas_call(
        kern,
        grid=(b, nc),
        in_specs=[col(qcol), col(fcol), col(vcol),
                  pl.BlockSpec((None, depth, hw), lambda bb, cc: (d, 0, 0))],
        out_specs=pl.BlockSpec((None, CHUNK, hw), lambda bb, cc: (bb, cidx(cc), 0)),
        out_shape=jax.ShapeDtypeStruct((b, s, hw), F32),
        scratch_shapes=[pltpu.VMEM((heads, hd, hd), F32)],
        compiler_params=_params(2),
        name="hgrn_scan_bwd" if rev else "hgrn_scan_fwd",
    )(proj3, proj3, proj3, lbraw)


def _outproj_kernel(oaf_ref, oab_ref, z_ref, obf_ref, obb_ref, g_ref, h_ref, gnw_ref, hnw_ref, wout_ref,
                    nfw_ref, wr_ref, hn_ref, u_ref, p_ref, mix_scr, *, ha, hb, hd, n_exp):
    def gated(of_ref, ob_ref, gate_ref, w_ref, heads, base):
        for hh in range(heads):
            sl = slice(hh * hd, (hh + 1) * hd)
            o = of_ref[:, sl] + ob_ref[:, sl]
            zz = gate_ref[:, sl]
            y = _rms(o, w_ref[...]) * (zz * jax.nn.sigmoid(zz))
            mix_scr[:, base + hh * hd:base + (hh + 1) * hd] = y.astype(BF16)

    gated(oaf_ref, oab_ref, z_ref, gnw_ref, ha, 0)
    gated(obf_ref, obb_ref, g_ref, hnw_ref, hb, ha * hd)
    hn = h_ref[...] + _dot(mix_scr[...], wout_ref[...])
    hn_ref[...] = hn
    u = _rms(hn, nfw_ref[...])
    u_ref[...] = u
    logits = jnp.dot(u, wr_ref[...], precision=lax.Precision.HIGHEST, preferred_element_type=F32)
    lane = lax.broadcasted_iota(I32, logits.shape, 1)
    logits = jnp.where(lane < n_exp, logits, -1e30)
    ex = jnp.exp(logits - jnp.max(logits, axis=-1, keepdims=True))
    ex = jnp.where(lane < n_exp, ex, 0.0)
    p_ref[...] = ex / jnp.sum(ex, axis=-1, keepdims=True)


def _out_projection(oaf, oab, obf, obb, proj, h, gnw, hnw, wout, nfw, wr_pad, ha, hb, hd, zcol, gcol, n_exp):
    t, d = h.shape
    gw, hw = ha * hd, hb * hd
    tm = _tile(t, 256)
    kern = functools.partial(_outproj_kernel, ha=ha, hb=hb, hd=hd, n_exp=n_exp)
    rowblk = lambda w: pl.BlockSpec((tm, w), lambda i: (i, 0))
    full = lambda shp: pl.BlockSpec(shp, lambda i: (0, 0))
    return pl.pallas_call(
        kern,
        grid=(t // tm,),
        in_specs=[rowblk(gw), rowblk(gw), pl.BlockSpec((tm, gw), lambda i: (i, zcol)),
                  rowblk(hw), rowblk(hw), pl.BlockSpec((tm, hw), lambda i: (i, gcol)),
                  rowblk(d), full((1, hd)), full((1, hd)), full((gw + hw, d)), full((1, d)),
                  full((d, LANES))],
        out_specs=[rowblk(d), rowblk(d), rowblk(LANES)],
        out_shape=[jax.ShapeDtypeStruct((t, d), F32), jax.ShapeDtypeStruct((t, d), F32),
                   jax.ShapeDtypeStruct((t, LANES), F32)],
        scratch_shapes=[pltpu.VMEM((tm, gw + hw), BF16)],
        compiler_params=_params(1),
        name="out_projection_router",
    )(oaf, oab, proj, obf, obb, proj, h, gnw, hnw, wout, nfw, wr_pad)


def _topk_kernel(p_ref, sp_ref, spt_ref, pst_ref, *, s, cap, n_exp):
    rb = _tile(s, 512)

    def count(pred):
        def body(i, acc):
            bits = pltpu.bitcast(p_ref[pl.ds(pl.multiple_of(i * rb, rb), rb), :], I32)
            return acc + jnp.sum(pred(bits).astype(I32).reshape(rb // SUBLANES, SUBLANES, LANES), axis=0)

        acc = lax.fori_loop(0, s // rb, body, jnp.zeros((SUBLANES, LANES), I32))
        return jnp.sum(acc, axis=0, keepdims=True)

    def bit_body(j, thr):
        cand = thr | jnp.left_shift(jnp.int32(1), 30 - j)
        return jnp.where(count(lambda bits: bits >= cand) >= cap, cand, thr)

    thr = lax.fori_loop(0, 31, bit_body, jnp.zeros((1, LANES), I32))
    need = (cap - count(lambda bits: bits > thr)).astype(F32)
    ri = lax.broadcasted_iota(I32, (LANES, LANES), 0)
    ci = lax.broadcasted_iota(I32, (LANES, LANES), 1)
    before = (ci < ri).astype(BF16)

    def chunk_body(cc, carry):
        cg, ce = carry
        r0 = pl.multiple_of(cc * LANES, LANES)
        bits = pltpu.bitcast(p_ref[pl.ds(r0, LANES), :], I32)
        gt = (bits > thr).astype(F32)
        eq = (bits == thr).astype(F32)
        pg = cg + _dot(before, gt.astype(BF16))
        pe = ce + _dot(before, eq.astype(BF16))
        sel = (gt > 0.0) | ((eq > 0.0) & (pe < need))
        pos = pg + jnp.minimum(pe, need)
        sp_ref[pl.ds(r0, LANES), :] = jnp.where(sel, pos, -1.0).astype(I32)
        spt = jnp.where(sel, pos, float(cap)).T
        spt_ref[:, pl.ds(r0, LANES)] = spt[0:n_exp, :].astype(I32)
        pst_ref[cc] = jnp.broadcast_to(cg + jnp.minimum(ce, need), (SUBLANES, LANES)).astype(I32)
        return cg + jnp.sum(gt, axis=0, keepdims=True), ce + jnp.sum(eq, axis=0, keepdims=True)

    zero = jnp.zeros((1, LANES), F32)
    lax.fori_loop(0, s // LANES, chunk_body, (zero, zero))


def _topk_positions(probs3, cap, n_exp):
    b, s, _ = probs3.shape
    kern = functools.partial(_topk_kernel, s=s, cap=cap, n_exp=n_exp)
    return pl.pallas_call(
        kern,
        grid=(b,),
        in_specs=[pl.BlockSpec((None, s, LANES), lambda bb: (bb, 0, 0))],
        out_specs=[pl.BlockSpec((None, s, LANES), lambda bb: (bb, 0, 0)),
                   pl.BlockSpec((None, n_exp, s), lambda bb: (bb, 0, 0)),
                   pl.BlockSpec((None, s // LANES, SUBLANES, LANES), lambda bb: (bb, 0, 0, 0))],
        out_shape=[jax.ShapeDtypeStruct((b, s, LANES), I32), jax.ShapeDtypeStruct((b, n_exp, s), I32),
                   jax.ShapeDtypeStruct((b, s // LANES, SUBLANES, LANES), I32)],
        compiler_params=_params(1),
        name="expert_choice_topk",
    )(probs3)


def _index_row_width(cap):
    return -(-(cap + 1) // SMEM_BLOCK) * SMEM_BLOCK


def _compact_kernel(spt_ref, idx_ref, *, s, cap):
    def clear(i, carry):
        idx_ref[cap + i] = 0
        return carry

    lax.fori_loop(0, idx_ref.shape[0] - cap, clear, 0)

    def body(t, carry):
        idx_ref[spt_ref[t]] = t
        return carry

    lax.fori_loop(0, s, body, 0, unroll=8)


def _compact_indices(spt_flat, n_rows, s, cap):
    kern = functools.partial(_compact_kernel, s=s, cap=cap)
    width = _index_row_width(cap)
    return pl.pallas_call(
        kern,
        grid=(n_rows,),
        in_specs=[pl.BlockSpec((s,), lambda i: (i,), memory_space=pltpu.SMEM)],
        out_specs=pl.BlockSpec((width,), lambda i: (i,), memory_space=pltpu.SMEM),
        out_shape=jax.ShapeDtypeStruct((n_rows * width,), I32),
        compiler_params=_params(1),
        name="expert_choice_compact",
    )(spt_flat)


def _row_copy(u_hbm, xs_scr, sem, src_row, dst_row):
    return pltpu.make_async_copy(u_hbm.at[pl.ds(src_row, 1)], xs_scr.at[pl.ds(dst_row, 1)], sem)


def _expert_kernel(idx_ref, u_hbm, wg_ref, wu_ref, wd_ref, y_ref, xs_scr, xb_scr, acc_scr, sem, *, s, ts):
    bb = pl.program_id(0)
    j = pl.program_id(2)
    fk = pl.program_id(3)

    @pl.when(fk == 0)
    def _():
        def issue(r, carry):
            _row_copy(u_hbm, xs_scr, sem, bb * s + idx_ref[j * ts + r], r).start()
            return carry

        lax.fori_loop(0, ts, issue, 0)

        def drain(r, carry):
            _row_copy(u_hbm, xs_scr, sem, 0, r).wait()
            return carry

        lax.fori_loop(0, ts, drain, 0)
        xb_scr[...] = xs_scr[...].astype(BF16)
        acc_scr[...] = jnp.zeros_like(acc_scr)

    xs = xb_scr[...]
    gate = _dot(xs, wg_ref[...])
    hid = (gate * jax.nn.sigmoid(gate) * _dot(xs, wu_ref[...])).astype(BF16)
    acc_scr[...] += _dot(hid, wd_ref[...])

    @pl.when(fk == pl.num_programs(3) - 1)
    def _():
        y_ref[...] = acc_scr[...].astype(y_ref.dtype)


def _experts(idx_flat, u, wg, wu, wd, b, s, cap):
    n_exp, d, f = wg.shape
    ts, tf = _tile(cap, 512), _tile(f, 512)
    width = _index_row_width(cap)
    kern = functools.partial(_expert_kernel, s=s, ts=ts)
    return pl.pallas_call(
        kern,
        grid=(b, n_exp, cap // ts, f // tf),
        in_specs=[pl.BlockSpec((width,), lambda bb, e, j, k: (bb * n_exp + e,), memory_space=pltpu.SMEM),
                  pl.BlockSpec(memory_space=pl.ANY),
                  pl.BlockSpec((None, d, tf), lambda bb, e, j, k: (e, 0, k)),
                  pl.BlockSpec((None, d, tf), lambda bb, e, j, k: (e, 0, k)),
                  pl.BlockSpec((None, tf, d), lambda bb, e, j, k: (e, k, 0))],
        out_specs=pl.BlockSpec((None, None, ts, d), lambda bb, e, j, k: (bb, e, j, 0)),
        out_shape=jax.ShapeDtypeStruct((b, n_exp, cap, d), BF16),
        scratch_shapes=[pltpu.VMEM((ts, d), F32), pltpu.VMEM((ts, d), BF16), pltpu.VMEM((ts, d), F32),
                        pltpu.SemaphoreType.DMA(())],
        compiler_params=_params(4),
        name="expert_ffn",
    )(idx_flat, u, wg, wu, wd)


def _window_copy(y_hbm, buf, sem, bb, e, start, window):
    return pltpu.make_async_copy(y_hbm.at[bb, e, pl.ds(start, window)], buf.at[e], sem.at[e])


def _combine_kernel(pst_ref, h_ref, sp_ref, p_ref, y_hbm, nw_ref, o_ref, buf, sem, *, n_exp, cap, window,
                    final):
    bb = pl.program_id(0)
    i = pl.program_id(1)
    base = (bb * pl.num_programs(1) + i) * n_exp
    starts = []
    for e in range(n_exp):
        first = pst_ref[base + e] & -BF16_ROWS
        start = pl.multiple_of(jnp.minimum(first, cap - window), BF16_ROWS)
        _window_copy(y_hbm, buf, sem, bb, e, start, window).start()
        starts.append(start)
    acc = h_ref[...]
    lane = lax.broadcasted_iota(I32, (acc.shape[0], window), 1)
    for e in range(n_exp):
        _window_copy(y_hbm, buf, sem, bb, e, starts[e], window).wait()
        onehot = (sp_ref[:, e:e + 1] - starts[e] == lane).astype(BF16)
        acc = acc + p_ref[:, e:e + 1] * _dot(onehot, buf[e])
    if final:
        acc = _rms(acc, nw_ref[...])
    o_ref[...] = acc


def _combine(pst_flat, h, sp, probs, y, nw, n_exp, cap, final):
    b, s, d = h.shape
    tm = LANES
    window = tm + BF16_ROWS
    kern = functools.partial(_combine_kernel, n_exp=n_exp, cap=cap, window=window, final=final)
    tok = lambda w: pl.BlockSpec((None, tm, w), lambda bb, i, pst: (bb, i, 0))
    grid_spec = pltpu.PrefetchScalarGridSpec(
        num_scalar_prefetch=1,
        grid=(b, s // tm),
        in_specs=[tok(d), tok(LANES), tok(LANES), pl.BlockSpec(memory_space=pl.ANY),
                  pl.BlockSpec((1, d), lambda bb, i, pst: (0, 0))],
        out_specs=tok(d),
        scratch_shapes=[pltpu.VMEM((n_exp, window, d), BF16), pltpu.SemaphoreType.DMA((n_exp,))],
    )
    return pl.pallas_call(
        kern,
        grid_spec=grid_spec,
        out_shape=jax.ShapeDtypeStruct((b, s, d), F32),
        compiler_params=_params(2),
        name="moe_combine",
    )(pst_flat, h, sp, probs, y, nw)


def _lane_row(vals):
    return jnp.zeros((1, LANES), F32).at[0, :vals.shape[0]].set(vals.astype(F32))


def kernel(x, norm_mix, norm_ffn, norm_final, w_in, conv_w, gdn_a_log, gdn_dt_bias, gdn_norm, hgrn_lower_bounds,
           hgrn_norm, w_out, w_router, w_gate, w_up, w_down):
    b, s, d = x.shape
    depth = w_in.shape[0]
    ha = gdn_a_log.shape[-1]
    hd = gdn_norm.shape[-1]
    gw = ha * hd
    hw = hgrn_lower_bounds.shape[-1]
    hb = hw // hd
    n_exp = w_router.shape[-1]
    cap = CAPACITY_FACTOR * s // n_exp
    assert hd == LANES and gw == hw and s % LANES == 0 and 4 * ha <= LANES and n_exp <= LANES
    assert cap >= LANES + BF16_ROWS and cap % BF16_ROWS == 0
    t = b * s
    sizes = (gw, gw, gw, gw, 2 * ha, 2 * ha, hw, 2 * hw, hw, hw)
    offs = [0]
    for sz in sizes:
        offs.append(offs[-1] + sz)
    lb_dir_major = jnp.transpose(hgrn_lower_bounds, (1, 0, 2))
    h = x.reshape(t, d)
    out = None
    for l in range(depth):
        wi = w_in[l]
        part = lambda n: wi[:, offs[n]:offs[n + 1]]
        w_main = jnp.concatenate([part(0), part(1), part(2), part(3), part(6), part(7), part(8), part(9)],
                                 axis=1).astype(BF16)
        w_small = jnp.zeros((d, LANES), F32).at[:, :4 * ha].set(
            jnp.concatenate([part(4), part(5)], axis=1)).astype(BF16)
        proj, gs = _in_projection(h, norm_mix[l][None, :], w_main, w_small)
        proj3 = proj.reshape(b, s, -1)
        gs3 = gs.reshape(b, s, LANES)
        qkv = _gdn_conv(proj3, conv_w[l], gw, hd)
        alog_row = jnp.zeros((1, LANES), F32).at[0, 2 * ha:4 * ha].set(gdn_a_log[l].reshape(-1))
        dt_row = jnp.zeros((1, LANES), F32).at[0, 2 * ha:4 * ha].set(gdn_dt_bias[l].reshape(-1))
        oa = [_gdn_scan(qkv, gs3, alog_row, dt_row, ha, hd, dd).reshape(t, gw) for dd in (0, 1)]
        ob = [_hgrn_scan(proj3, lb_dir_major, hb, hd, dd, l, 4, 5 + dd, 7).reshape(t, hw) for dd in (0, 1)]
        wr_pad = jnp.zeros((d, LANES), F32).at[:, :n_exp].set(w_router[l])
        hn, u, probs = _out_projection(oa[0], oa[1], ob[0], ob[1], proj, h, gdn_norm[l][None, :],
                                       hgrn_norm[l][None, :], w_out[l].astype(BF16), norm_ffn[l][None, :],
                                       wr_pad, ha, hb, hd, 3, 8, n_exp)
        sp, spt, pst = _topk_positions(probs.reshape(b, s, LANES), cap, n_exp)
        idx = _compact_indices(spt.reshape(-1), b * n_exp, s, cap)
        y = _experts(idx, u, w_gate[l].astype(BF16), w_up[l].astype(BF16), w_down[l].astype(BF16), b, s, cap)
        pst_flat = pst[:, :, 0, :n_exp].reshape(-1)
        out = _combine(pst_flat, hn.reshape(b, s, d), sp, probs.reshape(b, s, LANES), y, norm_final[None, :],
                       n_exp, cap, l == depth - 1)
        h = out.reshape(t, d)
    return out
```

```python
import functools

import jax
import jax.numpy as jnp
from jax import lax
from jax.experimental import pallas as pl
from jax.experimental.pallas import tpu as pltpu

F32, BF16, I32 = jnp.float32, jnp.bfloat16, jnp.int32
NORM_EPS = 1e-6
CHUNK = 64
CAPACITY_FACTOR = 2
LANES = 128
SUBLANES = 8
BF16_ROWS = 16
SMEM_BLOCK = 1024
VMEM_LIMIT = 56 * 1024 * 1024
NT_DIMS = (((1,), (1,)), ((), ()))
TN_DIMS = (((0,), (0,)), ((), ()))


def _tile(n, pref):
    return pref if n % pref == 0 else n


def _params(n_axes):
    return pltpu.CompilerParams(dimension_semantics=("arbitrary",) * n_axes,
                                vmem_limit_bytes=VMEM_LIMIT)


def _dot(a, b):
    return jnp.dot(a, b, preferred_element_type=F32)


def _dot_nt(a, b):
    return lax.dot_general(a, b, NT_DIMS, preferred_element_type=F32)


def _dot_tn(a, b):
    return lax.dot_general(a, b, TN_DIMS, preferred_element_type=F32)


def _rms(x, w):
    return x * lax.rsqrt(jnp.mean(x * x, axis=-1, keepdims=True) + NORM_EPS) * w


def _softplus(x):
    return jnp.maximum(x, 0.0) + jnp.log(1.0 + jnp.exp(-jnp.abs(x)))


def _cumsum_rows(x, rev):
    n = x.shape[0]
    row = lax.broadcasted_iota(I32, x.shape, 0)
    k = 1
    while k < n:
        if rev:
            x = x + jnp.where(row < n - k, pltpu.roll(x, n - k, 0), 0.0)
        else:
            x = x + jnp.where(row >= k, pltpu.roll(x, k, 0), 0.0)
        k *= 2
    return x


def _inproj_kernel(x_ref, nw_ref, w_ref, ws_ref, o_ref, os_ref, u_scr):
    @pl.when(pl.program_id(1) == 0)
    def _():
        ub = _rms(x_ref[...], nw_ref[...]).astype(BF16)
        u_scr[...] = ub
        os_ref[...] = _dot(ub, ws_ref[...])

    o_ref[...] = _dot(u_scr[...], w_ref[...])


def _in_projection(h, nw, w_main, w_small):
    t, d = h.shape
    n = w_main.shape[1]
    tm, tn = _tile(t, 1024), _tile(n, 1024)
    return pl.pallas_call(
        _inproj_kernel,
        grid=(t // tm, n // tn),
        in_specs=[pl.BlockSpec((tm, d), lambda i, j: (i, 0)),
                  pl.BlockSpec((1, d), lambda i, j: (0, 0)),
                  pl.BlockSpec((d, tn), lambda i, j: (0, j)),
                  pl.BlockSpec((d, LANES), lambda i, j: (0, 0))],
        out_specs=[pl.BlockSpec((tm, tn), lambda i, j: (i, j)),
                   pl.BlockSpec((tm, LANES), lambda i, j: (i, 0))],
        out_shape=[jax.ShapeDtypeStruct((t, n), F32), jax.ShapeDtypeStruct((t, LANES), F32)],
        scratch_shapes=[pltpu.VMEM((tm, d), BF16)],
        compiler_params=_params(2),
        name="in_projection",
    )(h, nw, w_main, w_small)


def _conv_kernel(prev_ref, x_ref, next_ref, w_ref, o_ref, ext_scr, *, rows, nblk, taps, hd):
    i = pl.program_id(1)
    g = pl.program_id(2)
    pad = SUBLANES
    ext_scr[0:pad, :] = jnp.where(i > 0, prev_ref[...], 0.0)
    ext_scr[pad:pad + rows, :] = x_ref[...]
    ext_scr[pad + rows:, :] = jnp.where(i < nblk - 1, next_ref[...], 0.0)
    first = pad - (taps - 1) // 2
    acc = ext_scr[pl.ds(first, rows), :] * w_ref[0:1, :]
    for j in range(1, taps):
        acc = acc + ext_scr[pl.ds(first + j, rows), :] * w_ref[j:j + 1, :]
    y = acc * jax.nn.sigmoid(acc)
    qscale = jnp.where(g == 0, hd ** -0.5, 1.0)
    for hh in range(y.shape[1] // hd):
        sl = slice(hh * hd, (hh + 1) * hd)
        yh = y[:, sl]
        r = lax.rsqrt(jnp.sum(yh * yh, axis=-1, keepdims=True) + NORM_EPS) * qscale
        o_ref[:, sl] = jnp.where(g == 2, yh, yh * r)


def _gdn_conv(proj3, conv_w, gw, hd):
    b, s, _ = proj3.shape
    taps = conv_w.shape[0]
    rows = _tile(s, 256)
    nblk = s // rows
    rb = rows // SUBLANES
    wpad = jnp.zeros((SUBLANES, 3 * gw), F32).at[:taps].set(conv_w)
    kern = functools.partial(_conv_kernel, rows=rows, nblk=nblk, taps=taps, hd=hd)
    return pl.pallas_call(
        kern,
        grid=(b, nblk, 3),
        in_specs=[pl.BlockSpec((None, SUBLANES, gw), lambda bb, i, g: (bb, jnp.maximum(i * rb - 1, 0), g)),
                  pl.BlockSpec((None, rows, gw), lambda bb, i, g: (bb, i, g)),
                  pl.BlockSpec((None, SUBLANES, gw),
                               lambda bb, i, g: (bb, jnp.minimum((i + 1) * rb, s // SUBLANES - 1), g)),
                  pl.BlockSpec((SUBLANES, gw), lambda bb, i, g: (0, g))],
        out_specs=pl.BlockSpec((None, rows, gw), lambda bb, i, g: (bb, i, g)),
        out_shape=jax.ShapeDtypeStruct((b, s, 3 * gw), F32),
        scratch_shapes=[pltpu.VMEM((rows + 2 * SUBLANES, gw), F32)],
        compiler_params=_params(3),
        name="gdn_conv",
    )(proj3, proj3, proj3, wpad)


def _head_group(heads):
    return 2 if heads % 2 == 0 else 1


def _pick(parts, block_id):
    out = parts[0]
    for i in range(1, len(parts)):
        out = jnp.where(block_id == i, parts[i], out)
    return out


def _gdn_kernel(q_ref, k_ref, v_ref, gs_ref, alog_ref, dt_ref, o_ref, s_scr, *, heads, hd, rev, d):
    c = CHUNK
    grp = _head_group(heads)
    r = grp * c
    groups = [list(range(g * grp, (g + 1) * grp)) for g in range(heads // grp)]
    shift = c.bit_length() - 1

    @pl.when(pl.program_id(1) == 0)
    def _():
        s_scr[...] = jnp.zeros_like(s_scr)

    gs = gs_ref[...]
    gall = -jnp.exp(alog_ref[...]) * _softplus(gs + dt_ref[...])
    gc1 = _cumsum_rows(gall, rev)
    gc = jnp.concatenate([gc1] * grp, axis=0)
    gct = gc.T
    bt = jnp.concatenate([jax.nn.sigmoid(gs)] * grp, axis=0)
    row_blk = lax.broadcasted_iota(I32, (r, 1), 0) >> shift
    lane_blk = lax.broadcasted_iota(I32, (1, r), 1) >> shift
    ti = lax.broadcasted_iota(I32, (r, r), 0)
    si = lax.broadcasted_iota(I32, (r, r), 1)
    same = (ti >> shift) == (si >> shift)
    tl = ti & (c - 1)
    sl = si & (c - 1)
    incl = same & ((tl <= sl) if rev else (tl >= sl))
    strict = same & ((tl < sl) if rev else (tl > sl))
    eye = (ti == si).astype(F32)
    last = 0 if rev else c - 1
    gcol = lambda h: 2 * heads + d * heads + h
    stack = lambda ref, hs: jnp.concatenate([ref[:, h * hd:(h + 1) * hd] for h in hs], axis=0)

    qs, ks, vs, decays, egs, kbs, gls, ys, ps, qks = [], [], [], [], [], [], [], [], [], []
    for hs in groups:
        gcc = _pick([gc[:, gcol(h):gcol(h) + 1] for h in hs], row_blk)
        gcr = _pick([gct[gcol(h):gcol(h) + 1, :] for h in hs], lane_blk)
        beta = _pick([bt[:, d * heads + h:d * heads + h + 1] for h in hs], row_blk)
        gl = _pick([gc1[last:last + 1, gcol(h):gcol(h) + 1] for h in hs], row_blk)
        q, k, v = stack(q_ref, hs), stack(k_ref, hs), stack(v_ref, hs)
        decay = jnp.where(incl, jnp.exp(jnp.minimum(gcc - gcr, 0.0)), 0.0)
        eg = jnp.exp(gcc)
        kb = k * beta
        kbf = k.astype(BF16)
        y = -jnp.where(strict, _dot_nt(kb.astype(BF16), kbf) * decay, 0.0)
        qks.append((_dot_nt(q.astype(BF16), kbf) * decay).astype(BF16))
        qs.append(q * eg)
        ks.append(k * jnp.exp(gl - gcc))
        vs.append(jnp.concatenate([v * beta, kb * eg], axis=1).astype(BF16))
        gls.append(gl)
        ys.append(y)
        ps.append(eye + y)

    span = 2
    while span < c:
        ybs = [y.astype(BF16) for y in ys]
        ys = [_dot(yb, yb) for yb in ybs]
        ps = [p + _dot(p.astype(BF16), y.astype(BF16)) for p, y in zip(ps, ys)]
        span *= 2

    uws = [_dot(p.astype(BF16), rhs) for p, rhs in zip(ps, vs)]

    states = [s_scr[h] for h in range(heads)]
    wss = []
    for g, hs in enumerate(groups):
        for i, h in enumerate(hs):
            rows = slice(i * c, (i + 1) * c)
            lhs = jnp.concatenate([uws[g][rows, hd:], qs[g][rows, :]], axis=0).astype(BF16)
            wss.append(_dot(lhs, states[h].astype(BF16)))
    for g, hs in enumerate(groups):
        vnb = jnp.concatenate([uws[g][i * c:(i + 1) * c, :hd] - wss[h][:c] for i, h in enumerate(hs)],
                              axis=0).astype(BF16)
        intra = _dot(qks[g], vnb)
        for i, h in enumerate(hs):
            rows = slice(i * c, (i + 1) * c)
            o_ref[:, h * hd:(h + 1) * hd] = wss[h][c:] + intra[rows]
            s_scr[h] = jnp.exp(gls[g][i * c:i * c + 1]) * states[h] + _dot_tn(
                ks[g][rows].astype(BF16), vnb[rows])


def _gdn_scan(qkv, gs3, alog_row, dt_row, heads, hd, d):
    b, s, _ = qkv.shape
    gw = heads * hd
    nc = s // CHUNK
    rev = d == 1
    cidx = (lambda cc: nc - 1 - cc) if rev else (lambda cc: cc)
    kern = functools.partial(_gdn_kernel, heads=heads, hd=hd, rev=rev, d=d)
    col = lambda g: pl.BlockSpec((None, CHUNK, gw), lambda bb, cc: (bb, cidx(cc), g))
    row = pl.BlockSpec((1, LANES), lambda bb, cc: (0, 0))
    return pl.pallas_call(
        kern,
        grid=(b, nc),
        in_specs=[col(0), col(1), col(2),
                  pl.BlockSpec((None, CHUNK, LANES), lambda bb, cc: (bb, cidx(cc), 0)), row, row],
        out_specs=pl.BlockSpec((None, CHUNK, gw), lambda bb, cc: (bb, cidx(cc), 0)),
        out_shape=jax.ShapeDtypeStruct((b, s, gw), F32),
        scratch_shapes=[pltpu.VMEM((heads, hd, hd), F32)],
        compiler_params=_params(2),
        name="gdn_scan_bwd" if rev else "gdn_scan_fwd",
    )(qkv, qkv, qkv, gs3, alog_row, dt_row)


def _hgrn_kernel(q_ref, f_ref, v_ref, lbraw_ref, o_ref, s_scr, *, heads, hd, rev, layer):
    c = CHUNK
    grp = _head_group(heads)
    r = grp * c
    groups = [list(range(g * grp, (g + 1) * grp)) for g in range(heads // grp)]

    @pl.when(pl.program_id(1) == 0)
    def _():
        s_scr[...] = jnp.zeros_like(s_scr)

    ti = lax.broadcasted_iota(I32, (r, r), 0)
    si = lax.broadcasted_iota(I32, (r, r), 1)
    row = lax.broadcasted_iota(I32, (r, hd), 0)
    raw = lbraw_ref[...]
    ex = jnp.exp(raw - jnp.max(raw, axis=0, keepdims=True))
    share = ex / jnp.sum(ex, axis=0, keepdims=True)
    lb_all = jnp.zeros((1, share.shape[1]), F32)
    for j in range(1, layer + 1):
        lb_all = lb_all + share[j:j + 1, :]
    tiny = jnp.finfo(F32).tiny
    stack = lambda ref, hs: jnp.concatenate([ref[:, h * hd:(h + 1) * hd] for h in hs], axis=0)

    qs, ks, vbs, ips, ess, tots, attns = [], [], [], [], [], [], []
    for hs in groups:
        q, fpre = stack(q_ref, hs), stack(f_ref, hs)
        lb = jnp.concatenate([jnp.broadcast_to(lb_all[:, h * hd:(h + 1) * hd], (c, hd)) for h in hs], axis=0)
        f = jnp.maximum(lb + (1.0 - lb) * jax.nn.sigmoid(fpre), tiny)
        k = (1.0 - lb) * jax.nn.sigmoid(-fpre)
        qs.append(q)
        ks.append(k)
        vbs.append(stack(v_ref, hs).astype(BF16))
        ips.append(f)
        ess.append(jnp.ones_like(f))
        tots.append(f)
        attns.append(jnp.where(ti == si, _dot_nt(q.astype(BF16), k.astype(BF16)), 0.0))
    m, shift = 1, 0
    while m < c:
        tb = ti >> shift
        sb = si >> shift
        if rev:
            pair = ((tb & 1) == 0) & (sb == tb + 1)
        else:
            pair = ((tb & 1) == 1) & (sb == tb - 1)
        upper = ((row >> shift) & 1) == 1
        for g in range(len(groups)):
            lvl = _dot_nt((qs[g] * ips[g]).astype(BF16), (ks[g] * ess[g]).astype(BF16))
            attns[g] = attns[g] + jnp.where(pair, lvl, 0.0)
        for g in range(len(groups)):
            below = pltpu.roll(tots[g], m, 0)
            above = pltpu.roll(tots[g], r - m, 0)
            if rev:
                ips[g] = ips[g] * jnp.where(upper, 1.0, above)
                ess[g] = ess[g] * jnp.where(upper, below, 1.0)
            else:
                ips[g] = ips[g] * jnp.where(upper, below, 1.0)
                ess[g] = ess[g] * jnp.where(upper, 1.0, above)
            tots[g] = tots[g] * jnp.where(upper, below, above)
        m, shift = 2 * m, shift + 1
    states = [s_scr[h] for h in range(heads)]
    intras = [_dot(attns[g].astype(BF16), vbs[g]) for g in range(len(groups))]
    for g, hs in enumerate(groups):
        qi = (qs[g] * ips[g]).astype(BF16)
        ke = (ks[g] * ess[g]).astype(BF16)
        for i, h in enumerate(hs):
            rows = slice(i * c, (i + 1) * c)
            o_ref[:, h * hd:(h + 1) * hd] = _dot_nt(qi[rows], states[h].astype(BF16)) + intras[g][rows]
            s_scr[h] = tots[g][i * c:i * c + 1, :] * states[h] + _dot_tn(vbs[g][rows], ke[rows])


def _hgrn_scan(proj3, lbraw, heads, hd, d, layer, qcol, fcol, vcol):
    b, s, _ = proj3.shape
    hw = heads * hd
    nc = s // CHUNK
    rev = d == 1
    depth = lbraw.shape[1]
    cidx = (lambda cc: nc - 1 - cc) if rev else (lambda cc: cc)
    kern = functools.partial(_hgrn_kernel, heads=heads, hd=hd, rev=rev, layer=layer)
    col = lambda g: pl.BlockSpec((None, CHUNK, hw), lambda bb, cc: (bb, cidx(cc), g))
    return pl.pallas_call(
        kern,
        grid=(b, nc),
        in_specs=[col(qcol), col(fcol), col(vcol),
                  pl.BlockSpec((None, depth, hw), lambda bb, cc: (d, 0, 0))],
        out_specs=pl.BlockSpec((None, CHUNK, hw), lambda bb, cc: (bb, cidx(cc), 0)),
        out_shape=jax.ShapeDtypeStruct((b, s, hw), F32),
        scratch_shapes=[pltpu.VMEM((heads, hd, hd), F32)],
        compiler_params=_params(2),
        name="hgrn_scan_bwd" if rev else "hgrn_scan_fwd",
    )(proj3, proj3, proj3, lbraw)


def _outproj_kernel(oaf_ref, oab_ref, z_ref, obf_ref, obb_ref, g_ref, h_ref, gnw_ref, hnw_ref, wout_ref,
                    nfw_ref, wr_ref, hn_ref, u_ref, p_ref, mix_scr, *, ha, hb, hd, n_exp):
    def gated(of_ref, ob_ref, gate_ref, w_ref, heads, base):
        for hh in range(heads):
            sl = slice(hh * hd, (hh + 1) * hd)
            o = of_ref[:, sl] + ob_ref[:, sl]
            zz = gate_ref[:, sl]
            y = _rms(o, w_ref[...]) * (zz * jax.nn.sigmoid(zz))
            mix_scr[:, base + hh * hd:base + (hh + 1) * hd] = y.astype(BF16)

    gated(oaf_ref, oab_ref, z_ref, gnw_ref, ha, 0)
    gated(obf_ref, obb_ref, g_ref, hnw_ref, hb, ha * hd)
    hn = h_ref[...] + _dot(mix_scr[...], wout_ref[...])
    hn_ref[...] = hn
    u = _rms(hn, nfw_ref[...])
    slab = u.shape[1] // LANES
    for cc in range(slab):
        u_ref[pl.ds(cc, u.shape[0], stride=slab), :] = u[:, cc * LANES:(cc + 1) * LANES]
    u_hi = u.astype(BF16)
    u_lo = (u - u_hi.astype(F32)).astype(BF16)
    hi_terms = _dot(u_hi, wr_ref[...])
    logits = hi_terms[:, :LANES] + hi_terms[:, LANES:] + _dot(u_lo, wr_ref[:, :LANES])
    lane = lax.broadcasted_iota(I32, logits.shape, 1)
    logits = jnp.where(lane < n_exp, logits, -1e30)
    ex = jnp.exp(logits - jnp.max(logits, axis=-1, keepdims=True))
    ex = jnp.where(lane < n_exp, ex, 0.0)
    p_ref[...] = ex / jnp.sum(ex, axis=-1, keepdims=True)


def _out_projection(oaf, oab, obf, obb, proj, h, gnw, hnw, wout, nfw, wr_pad, ha, hb, hd, zcol, gcol, n_exp):
    t, d = h.shape
    gw, hw = ha * hd, hb * hd
    tm = _tile(t, 256)
    kern = functools.partial(_outproj_kernel, ha=ha, hb=hb, hd=hd, n_exp=n_exp)
    rowblk = lambda w: pl.BlockSpec((tm, w), lambda i: (i, 0))
    full = lambda shp: pl.BlockSpec(shp, lambda i: (0, 0))
    return pl.pallas_call(
        kern,
        grid=(t // tm,),
        in_specs=[rowblk(gw), rowblk(gw), pl.BlockSpec((tm, gw), lambda i: (i, zcol)),
                  rowblk(hw), rowblk(hw), pl.BlockSpec((tm, hw), lambda i: (i, gcol)),
                  rowblk(d), full((1, hd)), full((1, hd)), full((gw + hw, d)), full((1, d)),
                  full((d, 2 * LANES))],
        out_specs=[rowblk(d), pl.BlockSpec((tm * (d // LANES), LANES), lambda i: (i, 0)), rowblk(LANES)],
        out_shape=[jax.ShapeDtypeStruct((t, d), F32), jax.ShapeDtypeStruct((t * (d // LANES), LANES), F32),
                   jax.ShapeDtypeStruct((t, LANES), F32)],
        scratch_shapes=[pltpu.VMEM((tm, gw + hw), BF16)],
        compiler_params=_params(1),
        name="out_projection_router",
    )(oaf, oab, proj, obf, obb, proj, h, gnw, hnw, wout, nfw, wr_pad)


def _topk_kernel(p_ref, sp_ref, spt_ref, pst_ref, *, s, cap, n_exp):
    rb = _tile(s, 512)

    def count(pred):
        def body(i, acc):
            bits = pltpu.bitcast(p_ref[pl.ds(pl.multiple_of(i * rb, rb), rb), :], I32)
            return acc + jnp.sum(pred(bits).astype(I32).reshape(rb // SUBLANES, SUBLANES, LANES), axis=0)

        acc = lax.fori_loop(0, s // rb, body, jnp.zeros((SUBLANES, LANES), I32))
        return jnp.sum(acc, axis=0, keepdims=True)

    def bit_body(j, thr):
        cand = thr | jnp.left_shift(jnp.int32(1), 30 - j)
        return jnp.where(count(lambda bits: bits >= cand) >= cap, cand, thr)

    thr = lax.fori_loop(0, 31, bit_body, jnp.zeros((1, LANES), I32))
    need = (cap - count(lambda bits: bits > thr)).astype(F32)
    ri = lax.broadcasted_iota(I32, (LANES, LANES), 0)
    ci = lax.broadcasted_iota(I32, (LANES, LANES), 1)
    before = (ci < ri).astype(BF16)

    def chunk_body(cc, carry):
        cg, ce = carry
        r0 = pl.multiple_of(cc * LANES, LANES)
        bits = pltpu.bitcast(p_ref[pl.ds(r0, LANES), :], I32)
        gt = (bits > thr).astype(F32)
        eq = (bits == thr).astype(F32)
        pg = cg + _dot(before, gt.astype(BF16))
        pe = ce + _dot(before, eq.astype(BF16))
        sel = (gt > 0.0) | ((eq > 0.0) & (pe < need))
        pos = pg + jnp.minimum(pe, need)
        sp_ref[pl.ds(r0, LANES), :] = jnp.where(sel, pos, -1.0).astype(I32)
        spt = jnp.where(sel, pos, float(cap)).T
        spt_ref[:, pl.ds(r0, LANES)] = spt[0:n_exp, :].astype(I32)
        pst_ref[cc] = jnp.broadcast_to(cg + jnp.minimum(ce, need), (SUBLANES, LANES)).astype(I32)
        return cg + jnp.sum(gt, axis=0, keepdims=True), ce + jnp.sum(eq, axis=0, keepdims=True)

    zero = jnp.zeros((1, LANES), F32)
    lax.fori_loop(0, s // LANES, chunk_body, (zero, zero))


def _topk_positions(probs3, cap, n_exp):
    b, s, _ = probs3.shape
    kern = functools.partial(_topk_kernel, s=s, cap=cap, n_exp=n_exp)
    return pl.pallas_call(
        kern,
        grid=(b,),
        in_specs=[pl.BlockSpec((None, s, LANES), lambda bb: (bb, 0, 0))],
        out_specs=[pl.BlockSpec((None, s, LANES), lambda bb: (bb, 0, 0)),
                   pl.BlockSpec((None, n_exp, s), lambda bb: (bb, 0, 0)),
                   pl.BlockSpec((None, s // LANES, SUBLANES, LANES), lambda bb: (bb, 0, 0, 0))],
        out_shape=[jax.ShapeDtypeStruct((b, s, LANES), I32), jax.ShapeDtypeStruct((b, n_exp, s), I32),
                   jax.ShapeDtypeStruct((b, s // LANES, SUBLANES, LANES), I32)],
        compiler_params=_params(1),
        name="expert_choice_topk",
    )(probs3)


def _index_row_width(cap):
    return -(-(cap + 1) // SMEM_BLOCK) * SMEM_BLOCK


def _compact_kernel(spt_ref, idx_ref, *, s, cap, n_exp):
    def clear(i, carry):
        idx_ref[cap + i] = 0
        return carry

    lax.fori_loop(0, idx_ref.shape[0] - cap, clear, 0)

    first_row = (pl.program_id(0) // n_exp) * s

    def body(t, carry):
        idx_ref[spt_ref[t]] = first_row + t
        return carry

    lax.fori_loop(0, s, body, 0, unroll=8)


def _compact_indices(spt_flat, n_rows, s, cap, n_exp):
    kern = functools.partial(_compact_kernel, s=s, cap=cap, n_exp=n_exp)
    width = _index_row_width(cap)
    return pl.pallas_call(
        kern,
        grid=(n_rows,),
        in_specs=[pl.BlockSpec((s,), lambda i: (i,), memory_space=pltpu.SMEM)],
        out_specs=pl.BlockSpec((width,), lambda i: (i,), memory_space=pltpu.SMEM),
        out_shape=jax.ShapeDtypeStruct((n_rows * width,), I32),
        compiler_params=_params(1),
        name="expert_choice_compact",
    )(spt_flat)


def _row_copy(u_hbm, xs_scr, sem, src_row, dst_row, slab):
    return pltpu.make_async_copy(u_hbm.at[pl.ds(src_row * slab, slab)], xs_scr.at[pl.ds(dst_row * slab, slab)], sem)


def _expert_kernel(idx_ref, u_hbm, wg_ref, wu_ref, wd_ref, y_ref, xs_scr, xb_scr, acc_scr, sem, *, ts, slab):
    j = pl.program_id(2)
    fk = pl.program_id(3)

    @pl.when(fk == 0)
    def _():
        def issue(r, carry):
            _row_copy(u_hbm, xs_scr, sem, idx_ref[j * ts + r], r, slab).start()
            return carry

        lax.fori_loop(0, ts, issue, 0, unroll=8)

        def drain(r, carry):
            _row_copy(u_hbm, xs_scr, sem, 0, r, slab).wait()
            return carry

        lax.fori_loop(0, ts, drain, 0, unroll=8)
        for cc in range(slab):
            xb_scr[:, cc * LANES:(cc + 1) * LANES] = xs_scr[pl.ds(cc, ts, stride=slab), :].astype(BF16)
        acc_scr[...] = jnp.zeros_like(acc_scr)

    xs = xb_scr[...]
    gate = _dot(xs, wg_ref[...])
    hid = (gate * jax.nn.sigmoid(gate) * _dot(xs, wu_ref[...])).astype(BF16)
    acc_scr[...] += _dot(hid, wd_ref[...])

    @pl.when(fk == pl.num_programs(3) - 1)
    def _():
        y_ref[...] = acc_scr[...].astype(y_ref.dtype)


def _experts(idx_flat, u_slabs, wg, wu, wd, b, cap):
    n_exp, d, f = wg.shape
    slab = d // LANES
    ts, tf = _tile(cap, 512), _tile(f, 512)
    width = _index_row_width(cap)
    kern = functools.partial(_expert_kernel, ts=ts, slab=slab)
    return pl.pallas_call(
        kern,
        grid=(b, n_exp, cap // ts, f // tf),
        in_specs=[pl.BlockSpec((width,), lambda bb, e, j, k: (bb * n_exp + e,), memory_space=pltpu.SMEM),
                  pl.BlockSpec(memory_space=pl.ANY),
                  pl.BlockSpec((None, d, tf), lambda bb, e, j, k: (e, 0, k)),
                  pl.BlockSpec((None, d, tf), lambda bb, e, j, k: (e, 0, k)),
                  pl.BlockSpec((None, tf, d), lambda bb, e, j, k: (e, k, 0))],
        out_specs=pl.BlockSpec((None, None, ts, d), lambda bb, e, j, k: (bb, e, j, 0)),
        out_shape=jax.ShapeDtypeStruct((b, n_exp, cap, d), BF16),
        scratch_shapes=[pltpu.VMEM((ts * slab, LANES), F32), pltpu.VMEM((ts, d), BF16), pltpu.VMEM((ts, d), F32),
                        pltpu.SemaphoreType.DMA(())],
        compiler_params=_params(4),
        name="expert_ffn",
    )(idx_flat, u_slabs, wg, wu, wd)


def _combine_kernel(pst_ref, h_ref, sp_ref, p_ref, y_hbm, nw_ref, o_ref, buf, sem, *, n_exp, cap, window,
                    final):
    n_tiles = pl.num_programs(1)
    step = pl.program_id(0) * n_tiles + pl.program_id(1)
    n_steps = pl.num_programs(0) * n_tiles

    def window_start(at_step, e):
        first = pst_ref[at_step * n_exp + e] & -BF16_ROWS
        return pl.multiple_of(jnp.minimum(first, cap - window), BF16_ROWS)

    def window_copy(at_step, e):
        half = at_step % 2
        src = y_hbm.at[at_step // n_tiles, e, pl.ds(window_start(at_step, e), window)]
        return pltpu.make_async_copy(src, buf.at[half, e], sem.at[half, e])

    @pl.when(step == 0)
    def _():
        for e in range(n_exp):
            window_copy(step, e).start()

    @pl.when(step + 1 < n_steps)
    def _():
        for e in range(n_exp):
            window_copy(step + 1, e).start()

    acc = h_ref[...]
    lane = lax.broadcasted_iota(I32, (acc.shape[0], window), 1)
    for e in range(n_exp):
        window_copy(step, e).wait()
        onehot = (sp_ref[:, e:e + 1] - window_start(step, e) == lane).astype(BF16)
        acc = acc + p_ref[:, e:e + 1] * _dot(onehot, buf[step % 2, e])
    if final:
        acc = _rms(acc, nw_ref[...])
    o_ref[...] = acc


def _combine(pst_flat, h, sp, probs, y, nw, n_exp, cap, final):
    b, s, d = h.shape
    tm = LANES
    window = tm + BF16_ROWS
    kern = functools.partial(_combine_kernel, n_exp=n_exp, cap=cap, window=window, final=final)
    tok = lambda w: pl.BlockSpec((None, tm, w), lambda bb, i, pst: (bb, i, 0))
    grid_spec = pltpu.PrefetchScalarGridSpec(
        num_scalar_prefetch=1,
        grid=(b, s // tm),
        in_specs=[tok(d), tok(LANES), tok(LANES), pl.BlockSpec(memory_space=pl.ANY),
                  pl.BlockSpec((1, d), lambda bb, i, pst: (0, 0))],
        out_specs=tok(d),
        scratch_shapes=[pltpu.VMEM((2, n_exp, window, d), BF16), pltpu.SemaphoreType.DMA((2, n_exp))],
    )
    return pl.pallas_call(
        kern,
        grid_spec=grid_spec,
        out_shape=jax.ShapeDtypeStruct((b, s, d), F32),
        compiler_params=_params(2),
        name="moe_combine",
    )(pst_flat, h, sp, probs, y, nw)


def _lane_row(vals):
    return jnp.zeros((1, LANES), F32).at[0, :vals.shape[0]].set(vals.astype(F32))


def kernel(x, norm_mix, norm_ffn, norm_final, w_in, conv_w, gdn_a_log, gdn_dt_bias, gdn_norm, hgrn_lower_bounds,
           hgrn_norm, w_out, w_router, w_gate, w_up, w_down):
    b, s, d = x.shape
    depth = w_in.shape[0]
    ha = gdn_a_log.shape[-1]
    hd = gdn_norm.shape[-1]
    gw = ha * hd
    hw = hgrn_lower_bounds.shape[-1]
    hb = hw // hd
    n_exp = w_router.shape[-1]
    cap = CAPACITY_FACTOR * s // n_exp
    assert hd == LANES and gw == hw and s % LANES == 0 and 4 * ha <= LANES and n_exp <= LANES
    assert cap >= LANES + BF16_ROWS and cap % BF16_ROWS == 0
    t = b * s
    sizes = (gw, gw, gw, gw, 2 * ha, 2 * ha, hw, 2 * hw, hw, hw)
    offs = [0]
    for sz in sizes:
        offs.append(offs[-1] + sz)
    lb_dir_major = jnp.transpose(hgrn_lower_bounds, (1, 0, 2))
    h = x.reshape(t, d)
    out = None
    for l in range(depth):
        wi = w_in[l]
        part = lambda n: wi[:, offs[n]:offs[n + 1]]
        w_main = jnp.concatenate([part(0), part(1), part(2), part(3), part(6), part(7), part(8), part(9)],
                                 axis=1).astype(BF16)
        w_small = jnp.zeros((d, LANES), F32).at[:, :4 * ha].set(
            jnp.concatenate([part(4), part(5)], axis=1)).astype(BF16)
        proj, gs = _in_projection(h, norm_mix[l][None, :], w_main, w_small)
        proj3 = proj.reshape(b, s, -1)
        gs3 = gs.reshape(b, s, LANES)
        qkv = _gdn_conv(proj3, conv_w[l], gw, hd)
        alog_row = jnp.zeros((1, LANES), F32).at[0, 2 * ha:4 * ha].set(gdn_a_log[l].reshape(-1))
        dt_row = jnp.zeros((1, LANES), F32).at[0, 2 * ha:4 * ha].set(gdn_dt_bias[l].reshape(-1))
        oa = [_gdn_scan(qkv, gs3, alog_row, dt_row, ha, hd, dd).reshape(t, gw) for dd in (0, 1)]
        ob = [_hgrn_scan(proj3, lb_dir_major, hb, hd, dd, l, 4, 5 + dd, 7).reshape(t, hw) for dd in (0, 1)]
        wr_f32 = jnp.zeros((d, LANES), F32).at[:, :n_exp].set(w_router[l])
        wr_hi = wr_f32.astype(BF16)
        wr_pad = jnp.concatenate([wr_hi, (wr_f32 - wr_hi.astype(F32)).astype(BF16)], axis=1)
        hn, u, probs = _out_projection(oa[0], oa[1], ob[0], ob[1], proj, h, gdn_norm[l][None, :],
                                       hgrn_norm[l][None, :], w_out[l].astype(BF16), norm_ffn[l][None, :],
                                       wr_pad, ha, hb, hd, 3, 8, n_exp)
        sp, spt, pst = _topk_positions(probs.reshape(b, s, LANES), cap, n_exp)
        idx = _compact_indices(spt.reshape(-1), b * n_exp, s, cap, n_exp)
        y = _experts(idx, u, w_gate[l].astype(BF16), w_up[l].astype(BF16), w_down[l].astype(BF16), b, cap)
        pst_flat = pst[:, :, 0, :n_exp].reshape(-1)
        out = _combine(pst_flat, hn.reshape(b, s, d), sp, probs.reshape(b, s, LANES), y, norm_final[None, :],
                       n_exp, cap, l == depth - 1)
        h = out.reshape(t, d)
    return out
```

```python
import functools

import jax
import jax.numpy as jnp
from jax import lax
from jax.experimental import pallas as pl
from jax.experimental.pallas import tpu as pltpu

F32, BF16, I32 = jnp.float32, jnp.bfloat16, jnp.int32
NORM_EPS = 1e-6
CHUNK = 64
CAPACITY_FACTOR = 2
LANES = 128
SUBLANES = 8
BF16_ROWS = 16
SMEM_BLOCK = 1024
COUNT_ROWS = 8 * SUBLANES
VMEM_LIMIT = 56 * 1024 * 1024
NT_DIMS = (((1,), (1,)), ((), ()))
TN_DIMS = (((0,), (0,)), ((), ()))


def _tile(n, pref):
    return pref if n % pref == 0 else n


def _params(n_axes):
    return pltpu.CompilerParams(dimension_semantics=("arbitrary",) * n_axes,
                                vmem_limit_bytes=VMEM_LIMIT)


def _dot(a, b):
    return jnp.dot(a, b, preferred_element_type=F32)


def _dot_nt(a, b):
    return lax.dot_general(a, b, NT_DIMS, preferred_element_type=F32)


def _dot_tn(a, b):
    return lax.dot_general(a, b, TN_DIMS, preferred_element_type=F32)


def _rms(x, w):
    return x * lax.rsqrt(jnp.mean(x * x, axis=-1, keepdims=True) + NORM_EPS) * w


def _softplus(x):
    return jnp.maximum(x, 0.0) + jnp.log(1.0 + jnp.exp(-jnp.abs(x)))


def _cumsum_rows(x, rev):
    n = x.shape[0]
    row = lax.broadcasted_iota(I32, x.shape, 0)
    k = 1
    while k < n:
        if rev:
            x = x + jnp.where(row < n - k, pltpu.roll(x, n - k, 0), 0.0)
        else:
            x = x + jnp.where(row >= k, pltpu.roll(x, k, 0), 0.0)
        k *= 2
    return x


def _inproj_kernel(x_ref, nw_ref, w_ref, ws_ref, o_ref, os_ref, u_scr):
    @pl.when(pl.program_id(1) == 0)
    def _():
        ub = _rms(x_ref[...], nw_ref[...]).astype(BF16)
        u_scr[...] = ub
        os_ref[...] = _dot(ub, ws_ref[...])

    o_ref[...] = _dot(u_scr[...], w_ref[...])


def _in_projection(h, nw, w_main, w_small):
    t, d = h.shape
    n = w_main.shape[1]
    tm, tn = _tile(t, 1024), _tile(n, 1024)
    return pl.pallas_call(
        _inproj_kernel,
        grid=(t // tm, n // tn),
        in_specs=[pl.BlockSpec((tm, d), lambda i, j: (i, 0)),
                  pl.BlockSpec((1, d), lambda i, j: (0, 0)),
                  pl.BlockSpec((d, tn), lambda i, j: (0, j)),
                  pl.BlockSpec((d, LANES), lambda i, j: (0, 0))],
        out_specs=[pl.BlockSpec((tm, tn), lambda i, j: (i, j)),
                   pl.BlockSpec((tm, LANES), lambda i, j: (i, 0))],
        out_shape=[jax.ShapeDtypeStruct((t, n), F32), jax.ShapeDtypeStruct((t, LANES), F32)],
        scratch_shapes=[pltpu.VMEM((tm, d), BF16)],
        compiler_params=_params(2),
        name="in_projection",
    )(h, nw, w_main, w_small)


def _conv_kernel(prev_ref, x_ref, next_ref, w_ref, o_ref, ext_scr, *, rows, nblk, taps, hd):
    i = pl.program_id(1)
    g = pl.program_id(2)
    pad = SUBLANES
    ext_scr[0:pad, :] = jnp.where(i > 0, prev_ref[...], 0.0)
    ext_scr[pad:pad + rows, :] = x_ref[...]
    ext_scr[pad + rows:, :] = jnp.where(i < nblk - 1, next_ref[...], 0.0)
    first = pad - (taps - 1) // 2
    acc = ext_scr[pl.ds(first, rows), :] * w_ref[0:1, :]
    for j in range(1, taps):
        acc = acc + ext_scr[pl.ds(first + j, rows), :] * w_ref[j:j + 1, :]
    y = acc * jax.nn.sigmoid(acc)
    qscale = jnp.where(g == 0, hd ** -0.5, 1.0)
    for hh in range(y.shape[1] // hd):
        sl = slice(hh * hd, (hh + 1) * hd)
        yh = y[:, sl]
        r = lax.rsqrt(jnp.sum(yh * yh, axis=-1, keepdims=True) + NORM_EPS) * qscale
        o_ref[:, sl] = jnp.where(g == 2, yh, yh * r)


def _gdn_conv(proj3, conv_w, gw, hd):
    b, s, _ = proj3.shape
    taps = conv_w.shape[0]
    rows = _tile(s, 256)
    nblk = s // rows
    rb = rows // SUBLANES
    wpad = jnp.zeros((SUBLANES, 3 * gw), F32).at[:taps].set(conv_w)
    kern = functools.partial(_conv_kernel, rows=rows, nblk=nblk, taps=taps, hd=hd)
    return pl.pallas_call(
        kern,
        grid=(b, nblk, 3),
        in_specs=[pl.BlockSpec((None, SUBLANES, gw), lambda bb, i, g: (bb, jnp.maximum(i * rb - 1, 0), g)),
                  pl.BlockSpec((None, rows, gw), lambda bb, i, g: (bb, i, g)),
                  pl.BlockSpec((None, SUBLANES, gw),
                               lambda bb, i, g: (bb, jnp.minimum((i + 1) * rb, s // SUBLANES - 1), g)),
                  pl.BlockSpec((SUBLANES, gw), lambda bb, i, g: (0, g))],
        out_specs=pl.BlockSpec((None, rows, gw), lambda bb, i, g: (bb, i, g)),
        out_shape=jax.ShapeDtypeStruct((b, s, 3 * gw), F32),
        scratch_shapes=[pltpu.VMEM((rows + 2 * SUBLANES, gw), F32)],
        compiler_params=_params(3),
        name="gdn_conv",
    )(proj3, proj3, proj3, wpad)


def _head_group(heads):
    return 2 if heads % 2 == 0 else 1


def _pick(parts, block_id):
    out = parts[0]
    for i in range(1, len(parts)):
        out = jnp.where(block_id == i, parts[i], out)
    return out


def _gdn_kernel(q_ref, k_ref, v_ref, gs_ref, alog_ref, dt_ref, o_ref, s_scr, *, heads, hd, rev, d, nsub):
    c = CHUNK
    grp = _head_group(heads)
    r = grp * c
    groups = [list(range(g * grp, (g + 1) * grp)) for g in range(heads // grp)]
    units = [(sc, g) for sc in range(nsub) for g in range(len(groups))]
    shift = c.bit_length() - 1

    @pl.when(pl.program_id(1) == 0)
    def _():
        s_scr[...] = jnp.zeros_like(s_scr)

    gs = gs_ref[...]
    gall = -jnp.exp(alog_ref[...]) * _softplus(gs + dt_ref[...])
    bt_all = jax.nn.sigmoid(gs)
    row_blk = lax.broadcasted_iota(I32, (r, 1), 0) >> shift
    lane_blk = lax.broadcasted_iota(I32, (1, r), 1) >> shift
    ti = lax.broadcasted_iota(I32, (r, r), 0)
    si = lax.broadcasted_iota(I32, (r, r), 1)
    same = (ti >> shift) == (si >> shift)
    tl = ti & (c - 1)
    sl = si & (c - 1)
    incl = same & ((tl <= sl) if rev else (tl >= sl))
    strict = same & ((tl < sl) if rev else (tl > sl))
    eye = (ti == si).astype(F32)
    last = 0 if rev else c - 1
    gcol = lambda h: 2 * heads + d * heads + h
    gc1s, gcs, gcts, bts = [], [], [], []
    for sc in range(nsub):
        gc1 = _cumsum_rows(gall[sc * c:(sc + 1) * c], rev)
        gc = jnp.concatenate([gc1] * grp, axis=0)
        gc1s.append(gc1)
        gcs.append(gc)
        gcts.append(gc.T)
        bts.append(jnp.concatenate([bt_all[sc * c:(sc + 1) * c]] * grp, axis=0))

    def stack(ref, sc, hs):
        return jnp.concatenate([ref[sc * c:(sc + 1) * c, h * hd:(h + 1) * hd] for h in hs], axis=0)

    qs, ks, vs, gls, ys, ps, qks = [], [], [], [], [], [], []
    for sc, g in units:
        hs = groups[g]
        gc, gct, bt = gcs[sc], gcts[sc], bts[sc]
        gcc = _pick([gc[:, gcol(h):gcol(h) + 1] for h in hs], row_blk)
        gcr = _pick([gct[gcol(h):gcol(h) + 1, :] for h in hs], lane_blk)
        beta = _pick([bt[:, d * heads + h:d * heads + h + 1] for h in hs], row_blk)
        gl = _pick([gc1s[sc][last:last + 1, gcol(h):gcol(h) + 1] for h in hs], row_blk)
        q, k, v = stack(q_ref, sc, hs), stack(k_ref, sc, hs), stack(v_ref, sc, hs)
        decay = jnp.where(incl, jnp.exp(jnp.minimum(gcc - gcr, 0.0)), 0.0)
        eg = jnp.exp(gcc)
        kb = k * beta
        kbf = k.astype(BF16)
        y = -jnp.where(strict, _dot_nt(kb.astype(BF16), kbf) * decay, 0.0)
        qks.append((_dot_nt(q.astype(BF16), kbf) * decay).astype(BF16))
        qs.append(q * eg)
        ks.append(k * jnp.exp(gl - gcc))
        vs.append(jnp.concatenate([v * beta, kb * eg], axis=1).astype(BF16))
        gls.append(gl)
        ys.append(y)
        ps.append(eye + y)

    span = 2
    while span < c:
        ybs = [y.astype(BF16) for y in ys]
        ys = [_dot(yb, yb) for yb in ybs]
        ps = [p + _dot(p.astype(BF16), y.astype(BF16)) for p, y in zip(ps, ys)]
        span *= 2

    uws = [_dot(p.astype(BF16), rhs) for p, rhs in zip(ps, vs)]

    states = [s_scr[h] for h in range(heads)]
    for sc in (range(nsub - 1, -1, -1) if rev else range(nsub)):
        wss = {}
        for g, hs in enumerate(groups):
            un = sc * len(groups) + g
            for i, h in enumerate(hs):
                rows = slice(i * c, (i + 1) * c)
                lhs = jnp.concatenate([uws[un][rows, hd:], qs[un][rows, :]], axis=0).astype(BF16)
                wss[h] = _dot(lhs, states[h].astype(BF16))
        for g, hs in enumerate(groups):
            un = sc * len(groups) + g
            vnb = jnp.concatenate([uws[un][i * c:(i + 1) * c, :hd] - wss[h][:c] for i, h in enumerate(hs)],
                                  axis=0).astype(BF16)
            intra = _dot(qks[un], vnb)
            for i, h in enumerate(hs):
                rows = slice(i * c, (i + 1) * c)
                o_ref[sc * c:(sc + 1) * c, h * hd:(h + 1) * hd] = wss[h][c:] + intra[rows]
                states[h] = jnp.exp(gls[un][i * c:i * c + 1]) * states[h] + _dot_tn(
                    ks[un][rows].astype(BF16), vnb[rows])
    for h in range(heads):
        s_scr[h] = states[h]


def _scan_chunks_per_step(n_chunks):
    return 4 if n_chunks % 4 == 0 else 1


def _gdn_scan(qkv, gs3, alog_row, dt_row, heads, hd, d):
    b, s, _ = qkv.shape
    gw = heads * hd
    nsub = _scan_chunks_per_step(s // CHUNK)
    rows = nsub * CHUNK
    nblk = s // rows
    rev = d == 1
    cidx = (lambda cc: nblk - 1 - cc) if rev else (lambda cc: cc)
    kern = functools.partial(_gdn_kernel, heads=heads, hd=hd, rev=rev, d=d, nsub=nsub)
    col = lambda g: pl.BlockSpec((None, rows, gw), lambda bb, cc: (bb, cidx(cc), g))
    row = pl.BlockSpec((1, LANES), lambda bb, cc: (0, 0))
    return pl.pallas_call(
        kern,
        grid=(b, nblk),
        in_specs=[col(0), col(1), col(2),
                  pl.BlockSpec((None, rows, LANES), lambda bb, cc: (bb, cidx(cc), 0)), row, row],
        out_specs=pl.BlockSpec((None, rows, gw), lambda bb, cc: (bb, cidx(cc), 0)),
        out_shape=jax.ShapeDtypeStruct((b, s, gw), F32),
        scratch_shapes=[pltpu.VMEM((heads, hd, hd), F32)],
        compiler_params=_params(2),
        name="gdn_scan_bwd" if rev else "gdn_scan_fwd",
    )(qkv, qkv, qkv, gs3, alog_row, dt_row)


def _hgrn_kernel(q_ref, f_ref, v_ref, lbraw_ref, o_ref, s_scr, *, heads, hd, rev, layer, nsub):
    c = CHUNK
    grp = _head_group(heads)
    r = grp * c
    groups = [list(range(g * grp, (g + 1) * grp)) for g in range(heads // grp)]
    units = [(sc, g) for sc in range(nsub) for g in range(len(groups))]

    @pl.when(pl.program_id(1) == 0)
    def _():
        s_scr[...] = jnp.zeros_like(s_scr)

    ti = lax.broadcasted_iota(I32, (r, r), 0)
    si = lax.broadcasted_iota(I32, (r, r), 1)
    row = lax.broadcasted_iota(I32, (r, hd), 0)
    raw = lbraw_ref[...]
    ex = jnp.exp(raw - jnp.max(raw, axis=0, keepdims=True))
    share = ex / jnp.sum(ex, axis=0, keepdims=True)
    lb_all = jnp.zeros((1, share.shape[1]), F32)
    for j in range(1, layer + 1):
        lb_all = lb_all + share[j:j + 1, :]
    tiny = jnp.finfo(F32).tiny

    def stack(ref, sc, hs):
        return jnp.concatenate([ref[sc * c:(sc + 1) * c, h * hd:(h + 1) * hd] for h in hs], axis=0)

    qs, ks, vbs, ips, ess, tots, attns = [], [], [], [], [], [], []
    for sc, g in units:
        hs = groups[g]
        q, fpre = stack(q_ref, sc, hs), stack(f_ref, sc, hs)
        lb = jnp.concatenate([jnp.broadcast_to(lb_all[:, h * hd:(h + 1) * hd], (c, hd)) for h in hs], axis=0)
        f = jnp.maximum(lb + (1.0 - lb) * jax.nn.sigmoid(fpre), tiny)
        k = (1.0 - lb) * jax.nn.sigmoid(-fpre)
        qs.append(q)
        ks.append(k)
        vbs.append(stack(v_ref, sc, hs).astype(BF16))
        ips.append(f)
        ess.append(jnp.ones_like(f))
        tots.append(f)
        attns.append(jnp.where(ti == si, _dot_nt(q.astype(BF16), k.astype(BF16)), 0.0))
    m, shift = 1, 0
    while m < c:
        tb = ti >> shift
        sb = si >> shift
        if rev:
            pair = ((tb & 1) == 0) & (sb == tb + 1)
        else:
            pair = ((tb & 1) == 1) & (sb == tb - 1)
        upper = ((row >> shift) & 1) == 1
        for un in range(len(units)):
            lvl = _dot_nt((qs[un] * ips[un]).astype(BF16), (ks[un] * ess[un]).astype(BF16))
            attns[un] = attns[un] + jnp.where(pair, lvl, 0.0)
        for un in range(len(units)):
            below = pltpu.roll(tots[un], m, 0)
            above = pltpu.roll(tots[un], r - m, 0)
            if rev:
                ips[un] = ips[un] * jnp.where(upper, 1.0, above)
                ess[un] = ess[un] * jnp.where(upper, below, 1.0)
            else:
                ips[un] = ips[un] * jnp.where(upper, below, 1.0)
                ess[un] = ess[un] * jnp.where(upper, 1.0, above)
            tots[un] = tots[un] * jnp.where(upper, below, above)
        m, shift = 2 * m, shift + 1
    intras = [_dot(attns[un].astype(BF16), vbs[un]) for un in range(len(units))]
    qis = [(qs[un] * ips[un]).astype(BF16) for un in range(len(units))]
    kes = [(ks[un] * ess[un]).astype(BF16) for un in range(len(units))]
    states = [s_scr[h] for h in range(heads)]
    for sc in (range(nsub - 1, -1, -1) if rev else range(nsub)):
        for g, hs in enumerate(groups):
            un = sc * len(groups) + g
            for i, h in enumerate(hs):
                rows = slice(i * c, (i + 1) * c)
                o_ref[sc * c:(sc + 1) * c, h * hd:(h + 1) * hd] = _dot_nt(
                    qis[un][rows], states[h].astype(BF16)) + intras[un][rows]
                states[h] = tots[un][i * c:i * c + 1, :] * states[h] + _dot_tn(vbs[un][rows], kes[un][rows])
    for h in range(heads):
        s_scr[h] = states[h]


def _hgrn_scan(proj3, lbraw, heads, hd, d, layer, qcol, fcol, vcol):
    b, s, _ = proj3.shape
    hw = heads * hd
    nsub = _scan_chunks_per_step(s // CHUNK)
    rows = nsub * CHUNK
    nblk = s // rows
    rev = d == 1
    depth = lbraw.shape[1]
    cidx = (lambda cc: nblk - 1 - cc) if rev else (lambda cc: cc)
    kern = functools.partial(_hgrn_kernel, heads=heads, hd=hd, rev=rev, layer=layer, nsub=nsub)
    col = lambda g: pl.BlockSpec((None, rows, hw), lambda bb, cc: (bb, cidx(cc), g))
    return pl.pallas_call(
        kern,
        grid=(b, nblk),
        in_specs=[col(qcol), col(fcol), col(vcol),
                  pl.BlockSpec((None, depth, hw), lambda bb, cc: (d, 0, 0))],
        out_specs=pl.BlockSpec((None, rows, hw), lambda bb, cc: (bb, cidx(cc), 0)),
        out_shape=jax.ShapeDtypeStruct((b, s, hw), F32),
        scratch_shapes=[pltpu.VMEM((heads, hd, hd), F32)],
        compiler_params=_params(2),
        name="hgrn_scan_bwd" if rev else "hgrn_scan_fwd",
    )(proj3, proj3, proj3, lbraw)


def _outproj_kernel(oaf_ref, oab_ref, z_ref, obf_ref, obb_ref, g_ref, h_ref, gnw_ref, hnw_ref, wout_ref,
                    nfw_ref, wr_ref, hn_ref, u_ref, p_ref, mix_scr, *, ha, hb, hd, n_exp):
    def gated(of_ref, ob_ref, gate_ref, w_ref, heads, base):
        for hh in range(heads):
            sl = slice(hh * hd, (hh + 1) * hd)
            o = of_ref[:, sl] + ob_ref[:, sl]
            zz = gate_ref[:, sl]
            y = _rms(o, w_ref[...]) * (zz * jax.nn.sigmoid(zz))
            mix_scr[:, base + hh * hd:base + (hh + 1) * hd] = y.astype(BF16)

    gated(oaf_ref, oab_ref, z_ref, gnw_ref, ha, 0)
    gated(obf_ref, obb_ref, g_ref, hnw_ref, hb, ha * hd)
    hn = h_ref[...] + _dot(mix_scr[...], wout_ref[...])
    hn_ref[...] = hn
    u = _rms(hn, nfw_ref[...])
    u_hi = u.astype(BF16)
    u_lo = (u - u_hi.astype(F32)).astype(BF16)
    hi_terms = _dot(u_hi, wr_ref[...])
    logits = hi_terms[:, :LANES] + hi_terms[:, LANES:] + _dot(u_lo, wr_ref[:, :LANES])
    lane = lax.broadcasted_iota(I32, logits.shape, 1)
    logits = jnp.where(lane < n_exp, logits, -1e30)
    ex = jnp.exp(logits - jnp.max(logits, axis=-1, keepdims=True))
    ex = jnp.where(lane < n_exp, ex, 0.0)
    probs = ex / jnp.sum(ex, axis=-1, keepdims=True)
    p_ref[...] = probs
    data_rows = u.shape[1] // LANES
    slab = data_rows + SUBLANES
    for cc in range(data_rows):
        u_ref[pl.ds(cc, u.shape[0], stride=slab), :] = u[:, cc * LANES:(cc + 1) * LANES]
    for cc in range(data_rows, slab):
        u_ref[pl.ds(cc, u.shape[0], stride=slab), :] = probs


def _slab_rows(d):
    return d // LANES + SUBLANES


def _out_projection(oaf, oab, obf, obb, proj, h, gnw, hnw, wout, nfw, wr_pad, ha, hb, hd, zcol, gcol, n_exp):
    t, d = h.shape
    gw, hw = ha * hd, hb * hd
    tm = _tile(t, 256)
    kern = functools.partial(_outproj_kernel, ha=ha, hb=hb, hd=hd, n_exp=n_exp)
    rowblk = lambda w: pl.BlockSpec((tm, w), lambda i: (i, 0))
    full = lambda shp: pl.BlockSpec(shp, lambda i: (0, 0))
    return pl.pallas_call(
        kern,
        grid=(t // tm,),
        in_specs=[rowblk(gw), rowblk(gw), pl.BlockSpec((tm, gw), lambda i: (i, zcol)),
                  rowblk(hw), rowblk(hw), pl.BlockSpec((tm, hw), lambda i: (i, gcol)),
                  rowblk(d), full((1, hd)), full((1, hd)), full((gw + hw, d)), full((1, d)),
                  full((d, 2 * LANES))],
        out_specs=[rowblk(d), pl.BlockSpec((tm * _slab_rows(d), LANES), lambda i: (i, 0)), rowblk(LANES)],
        out_shape=[jax.ShapeDtypeStruct((t, d), F32), jax.ShapeDtypeStruct((t * _slab_rows(d), LANES), F32),
                   jax.ShapeDtypeStruct((t, LANES), F32)],
        scratch_shapes=[pltpu.VMEM((tm, gw + hw), BF16)],
        compiler_params=_params(1),
        name="out_projection_router",
    )(oaf, oab, proj, obf, obb, proj, h, gnw, hnw, wout, nfw, wr_pad)


def _index_row_width(cap):
    return -(-cap // SMEM_BLOCK) * SMEM_BLOCK


def _topk_kernel(p_ref, sp_ref, idx_ref, pst_ref, incl_scr, bc_scr, *, s, cap, n_exp):
    rb = _tile(s, 512)

    def count(pred):
        def body(i, acc):
            bits = pltpu.bitcast(p_ref[pl.ds(pl.multiple_of(i * rb, rb), rb), :], I32)
            return acc + jnp.sum(pred(bits).astype(I32).reshape(rb // SUBLANES, SUBLANES, LANES), axis=0)

        acc = lax.fori_loop(0, s // rb, body, jnp.zeros((SUBLANES, LANES), I32))
        return jnp.sum(acc, axis=0, keepdims=True)

    def bit_body(j, thr):
        cand = thr | jnp.left_shift(jnp.int32(1), 30 - j)
        return jnp.where(count(lambda bits: bits >= cand) >= cap, cand, thr)

    thr = lax.fori_loop(0, 31, bit_body, jnp.zeros((1, LANES), I32))
    need = (cap - count(lambda bits: bits > thr)).astype(F32)
    ri = lax.broadcasted_iota(I32, (LANES, LANES), 0)
    ci = lax.broadcasted_iota(I32, (LANES, LANES), 1)
    before = (ci < ri).astype(BF16)

    def chunk_body(cc, carry):
        cg, ce = carry
        r0 = pl.multiple_of(cc * LANES, LANES)
        bits = pltpu.bitcast(p_ref[pl.ds(r0, LANES), :], I32)
        gt = (bits > thr).astype(F32)
        eq = (bits == thr).astype(F32)
        pg = cg + _dot(before, gt.astype(BF16))
        pe = ce + _dot(before, eq.astype(BF16))
        sel = (gt > 0.0) | ((eq > 0.0) & (pe < need))
        pos = pg + jnp.minimum(pe, need)
        sp_ref[pl.ds(r0, LANES), :] = jnp.where(sel, pos, -1.0).astype(I32)
        incl_scr[pl.ds(r0, LANES), :] = pos + jnp.where(sel, 1.0, 0.0)
        pst_ref[cc] = jnp.broadcast_to(cg + jnp.minimum(ce, need), (SUBLANES, LANES)).astype(I32)
        return cg + jnp.sum(gt, axis=0, keepdims=True), ce + jnp.sum(eq, axis=0, keepdims=True)

    zero = jnp.zeros((1, LANES), F32)
    lax.fori_loop(0, s // LANES, chunk_body, (zero, zero))

    idx_ref[...] = jnp.zeros_like(idx_ref)
    first_row = pl.program_id(0) * s
    lane_f = lax.broadcasted_iota(I32, (1, LANES), 1).astype(F32)
    for e in range(n_exp):
        def fill(i, carry):
            r0 = pl.multiple_of(i * rb, rb)
            bc_scr[pl.ds(r0, rb), :] = jnp.broadcast_to(incl_scr[pl.ds(r0, rb), e:e + 1], (rb, LANES))
            return carry

        lax.fori_loop(0, s // rb, fill, 0)

        def slot_tile(jt, carry):
            slot = lane_f + lax.convert_element_type(jt * LANES, F32)

            def body(i, acc):
                blk = bc_scr[pl.ds(pl.multiple_of(i * rb, rb), rb), :]
                return acc + jnp.sum(jnp.where(blk <= slot, 1.0, 0.0).reshape(-1, COUNT_ROWS, LANES), axis=0)

            acc = lax.fori_loop(0, s // rb, body, jnp.zeros((COUNT_ROWS, LANES), F32))
            token = jnp.sum(acc, axis=0, keepdims=True).astype(I32)
            idx_ref[e:e + 1, pl.ds(pl.multiple_of(jt * LANES, LANES), LANES)] = token + first_row
            return carry

        lax.fori_loop(0, cap // LANES, slot_tile, 0)


def _topk_positions(probs3, cap, n_exp):
    b, s, _ = probs3.shape
    width = _index_row_width(cap)
    kern = functools.partial(_topk_kernel, s=s, cap=cap, n_exp=n_exp)
    return pl.pallas_call(
        kern,
        grid=(b,),
        in_specs=[pl.BlockSpec((None, s, LANES), lambda bb: (bb, 0, 0))],
        out_specs=[pl.BlockSpec((None, s, LANES), lambda bb: (bb, 0, 0)),
                   pl.BlockSpec((None, n_exp, width), lambda bb: (bb, 0, 0)),
                   pl.BlockSpec((None, s // LANES, SUBLANES, LANES), lambda bb: (bb, 0, 0, 0))],
        out_shape=[jax.ShapeDtypeStruct((b, s, LANES), I32), jax.ShapeDtypeStruct((b, n_exp, width), I32),
                   jax.ShapeDtypeStruct((b, s // LANES, SUBLANES, LANES), I32)],
        scratch_shapes=[pltpu.VMEM((s, LANES), F32), pltpu.VMEM((s, LANES), F32)],
        compiler_params=_params(1),
        name="expert_choice_topk",
    )(probs3)


def _row_copy(u_hbm, xs_scr, sem, src_row, dst_row, slab):
    return pltpu.make_async_copy(u_hbm.at[pl.ds(src_row * slab, slab)], xs_scr.at[pl.ds(dst_row * slab, slab)], sem)


def _expert_kernel(idx_ref, idx_next_ref, u_hbm, wg_ref, wu_ref, wd_ref, y_ref, xs_scr, xb_scr, gate_scr, acc_scr,
                   sem, *, ts, slab):
    e = pl.program_id(1)
    j = pl.program_id(2)
    fk = pl.program_id(3)
    n_j = pl.num_programs(2)
    tile = (pl.program_id(0) * pl.num_programs(1) + e) * n_j + j
    n_tiles = pl.num_programs(0) * pl.num_programs(1) * n_j
    data_rows = slab - SUBLANES

    def issue(ref, base):
        def body(r, carry):
            _row_copy(u_hbm, xs_scr, sem, ref[base + r], r, slab).start()
            return carry

        lax.fori_loop(0, ts, body, 0, unroll=8)

    @pl.when(fk == 0)
    def _():
        @pl.when(tile == 0)
        def _():
            issue(idx_ref, 0)

        def drain(r, carry):
            _row_copy(u_hbm, xs_scr, sem, 0, r, slab).wait()
            return carry

        lax.fori_loop(0, ts, drain, 0, unroll=8)
        for cc in range(data_rows):
            xb_scr[:, cc * LANES:(cc + 1) * LANES] = xs_scr[pl.ds(cc, ts, stride=slab), :].astype(BF16)
        probs = xs_scr[pl.ds(data_rows, ts, stride=slab), :]
        lane = lax.broadcasted_iota(I32, probs.shape, 1)
        gate_scr[...] = jnp.sum(jnp.where(lane == e, probs, 0.0), axis=1, keepdims=True)
        acc_scr[...] = jnp.zeros_like(acc_scr)

        @pl.when(j + 1 < n_j)
        def _():
            issue(idx_ref, (j + 1) * ts)

        @pl.when((j + 1 == n_j) & (tile + 1 < n_tiles))
        def _():
            issue(idx_next_ref, 0)

    xs = xb_scr[...]
    gate = _dot(xs, wg_ref[...])
    hid = (gate * jax.nn.sigmoid(gate) * _dot(xs, wu_ref[...])).astype(BF16)
    acc_scr[...] += _dot(hid, wd_ref[...])

    @pl.when(fk == pl.num_programs(3) - 1)
    def _():
        y_ref[...] = (acc_scr[...] * gate_scr[...]).astype(y_ref.dtype)


def _experts(idx_flat, u_slabs, wg, wu, wd, b, cap):
    n_exp, d, f = wg.shape
    slab = _slab_rows(d)
    ts, tf = _tile(cap, 512), _tile(f, 512)
    width = _index_row_width(cap)
    last = b * n_exp - 1
    kern = functools.partial(_expert_kernel, ts=ts, slab=slab)
    return pl.pallas_call(
        kern,
        grid=(b, n_exp, cap // ts, f // tf),
        in_specs=[pl.BlockSpec((width,), lambda bb, e, j, k: (bb * n_exp + e,), memory_space=pltpu.SMEM),
                  pl.BlockSpec((width,), lambda bb, e, j, k: (jnp.minimum(bb * n_exp + e + 1, last),),
                               memory_space=pltpu.SMEM),
                  pl.BlockSpec(memory_space=pl.ANY),
                  pl.BlockSpec((None, d, tf), lambda bb, e, j, k: (e, 0, k)),
                  pl.BlockSpec((None, d, tf), lambda bb, e, j, k: (e, 0, k)),
                  pl.BlockSpec((None, tf, d), lambda bb, e, j, k: (e, k, 0))],
        out_specs=pl.BlockSpec((None, None, ts, d), lambda bb, e, j, k: (bb, e, j, 0)),
        out_shape=jax.ShapeDtypeStruct((b, n_exp, cap, d), BF16),
        scratch_shapes=[pltpu.VMEM((ts * slab, LANES), F32), pltpu.VMEM((ts, d), BF16), pltpu.VMEM((ts, 1), F32),
                        pltpu.VMEM((ts, d), F32), pltpu.SemaphoreType.DMA(())],
        compiler_params=_params(4),
        name="expert_ffn",
    )(idx_flat, idx_flat, u_slabs, wg, wu, wd)


def _combine_kernel(pst_ref, h_ref, sp_ref, y_hbm, nw_ref, o_ref, buf, sem, *, n_exp, cap, window, final):
    n_tiles = pl.num_programs(1)
    step = pl.program_id(0) * n_tiles + pl.program_id(1)
    n_steps = pl.num_programs(0) * n_tiles

    def window_start(at_step, e):
        first = pst_ref[at_step * n_exp + e] & -BF16_ROWS
        return pl.multiple_of(jnp.minimum(first, cap - window), BF16_ROWS)

    def window_copy(at_step, e):
        half = at_step % 2
        src = y_hbm.at[at_step // n_tiles, e, pl.ds(window_start(at_step, e), window)]
        return pltpu.make_async_copy(src, buf.at[half, pl.ds(e * window, window)], sem.at[half, e])

    @pl.when(step == 0)
    def _():
        for e in range(n_exp):
            window_copy(step, e).start()

    @pl.when(step + 1 < n_steps)
    def _():
        for e in range(n_exp):
            window_copy(step + 1, e).start()

    tm = h_ref.shape[0]
    lane = lax.broadcasted_iota(I32, (tm, LANES), 1)
    hits = [None] * (n_exp * window // LANES)
    for e in range(n_exp):
        window_copy(step, e).wait()
        slot = sp_ref[:, e:e + 1]
        target = jnp.where(slot >= 0, slot - window_start(step, e) + e * window, -1)
        for col in range(e * window // LANES, (e * window + window - 1) // LANES + 1):
            hit = target == lane + col * LANES
            hits[col] = hit if hits[col] is None else hits[col] | hit
    onehot = jnp.concatenate([jnp.where(hit, 1.0, 0.0).astype(BF16) for hit in hits], axis=1)
    acc = h_ref[...] + _dot(onehot, buf[step % 2])
    if final:
        acc = _rms(acc, nw_ref[...])
    o_ref[...] = acc


def _combine(pst_flat, h, sp, y, nw, n_exp, cap, final):
    b, s, d = h.shape
    tm = LANES
    window = tm + BF16_ROWS
    assert (n_exp * window) % LANES == 0
    kern = functools.partial(_combine_kernel, n_exp=n_exp, cap=cap, window=window, final=final)
    tok = lambda w: pl.BlockSpec((None, tm, w), lambda bb, i, pst: (bb, i, 0))
    grid_spec = pltpu.PrefetchScalarGridSpec(
        num_scalar_prefetch=1,
        grid=(b, s // tm),
        in_specs=[tok(d), tok(LANES), pl.BlockSpec(memory_space=pl.ANY),
                  pl.BlockSpec((1, d), lambda bb, i, pst: (0, 0))],
        out_specs=tok(d),
        scratch_shapes=[pltpu.VMEM((2, n_exp * window, d), BF16), pltpu.SemaphoreType.DMA((2, n_exp))],
    )
    return pl.pallas_call(
        kern,
        grid_spec=grid_spec,
        out_shape=jax.ShapeDtypeStruct((b, s, d), F32),
        compiler_params=_params(2),
        name="moe_combine",
    )(pst_flat, h, sp, y, nw)


def _lane_row(vals):
    return jnp.zeros((1, LANES), F32).at[0, :vals.shape[0]].set(vals.astype(F32))


def kernel(x, norm_mix, norm_ffn, norm_final, w_in, conv_w, gdn_a_log, gdn_dt_bias, gdn_norm, hgrn_lower_bounds,
           hgrn_norm, w_out, w_router, w_gate, w_up, w_down):
    b, s, d = x.shape
    depth = w_in.shape[0]
    ha = gdn_a_log.shape[-1]
    hd = gdn_norm.shape[-1]
    gw = ha * hd
    hw = hgrn_lower_bounds.shape[-1]
    hb = hw // hd
    n_exp = w_router.shape[-1]
    cap = CAPACITY_FACTOR * s // n_exp
    assert hd == LANES and gw == hw and s % LANES == 0 and 4 * ha <= LANES and n_exp <= LANES
    assert cap >= LANES + BF16_ROWS and cap % LANES == 0
    t = b * s
    sizes = (gw, gw, gw, gw, 2 * ha, 2 * ha, hw, 2 * hw, hw, hw)
    offs = [0]
    for sz in sizes:
        offs.append(offs[-1] + sz)
    lb_dir_major = jnp.transpose(hgrn_lower_bounds, (1, 0, 2))
    h = x.reshape(t, d)
    out = None
    for l in range(depth):
        wi = w_in[l]
        part = lambda n: wi[:, offs[n]:offs[n + 1]]
        w_main = jnp.concatenate([part(0), part(1), part(2), part(3), part(6), part(7), part(8), part(9)],
                                 axis=1).astype(BF16)
        w_small = jnp.zeros((d, LANES), F32).at[:, :4 * ha].set(
            jnp.concatenate([part(4), part(5)], axis=1)).astype(BF16)
        proj, gs = _in_projection(h, norm_mix[l][None, :], w_main, w_small)
        proj3 = proj.reshape(b, s, -1)
        gs3 = gs.reshape(b, s, LANES)
        qkv = _gdn_conv(proj3, conv_w[l], gw, hd)
        alog_row = jnp.zeros((1, LANES), F32).at[0, 2 * ha:4 * ha].set(gdn_a_log[l].reshape(-1))
        dt_row = jnp.zeros((1, LANES), F32).at[0, 2 * ha:4 * ha].set(gdn_dt_bias[l].reshape(-1))
        oa = [_gdn_scan(qkv, gs3, alog_row, dt_row, ha, hd, dd).reshape(t, gw) for dd in (0, 1)]
        ob = [_hgrn_scan(proj3, lb_dir_major, hb, hd, dd, l, 4, 5 + dd, 7).reshape(t, hw) for dd in (0, 1)]
        wr_f32 = jnp.zeros((d, LANES), F32).at[:, :n_exp].set(w_router[l])
        wr_hi = wr_f32.astype(BF16)
        wr_pad = jnp.concatenate([wr_hi, (wr_f32 - wr_hi.astype(F32)).astype(BF16)], axis=1)
        hn, u, probs = _out_projection(oa[0], oa[1], ob[0], ob[1], proj, h, gdn_norm[l][None, :],
                                       hgrn_norm[l][None, :], w_out[l].astype(BF16), norm_ffn[l][None, :],
                                       wr_pad, ha, hb, hd, 3, 8, n_exp)
        sp, idx, pst = _topk_positions(probs.reshape(b, s, LANES), cap, n_exp)
        y = _experts(idx.reshape(-1), u, w_gate[l].astype(BF16), w_up[l].astype(BF16), w_down[l].astype(BF16),
                     b, cap)
        pst_flat = pst[:, :, 0, :n_exp].reshape(-1)
        out = _combine(pst_flat, hn.reshape(b, s, d), sp, y, norm_final[None, :], n_exp, cap, l == depth - 1)
        h = out.reshape(t, d)
    return out
```

```python
import functools

import jax
import jax.numpy as jnp
from jax import lax
from jax.experimental import pallas as pl
from jax.experimental.pallas import tpu as pltpu

F32, BF16, I32 = jnp.float32, jnp.bfloat16, jnp.int32
NORM_EPS = 1e-6
CHUNK = 64
CAPACITY_FACTOR = 2
LANES = 128
SUBLANES = 8
BF16_ROWS = 16
SMEM_BLOCK = 1024
COUNT_ROWS = 8 * SUBLANES
VMEM_LIMIT = 56 * 1024 * 1024
NT_DIMS = (((1,), (1,)), ((), ()))
TN_DIMS = (((0,), (0,)), ((), ()))


def _tile(n, pref):
    return pref if n % pref == 0 else n


def _params(n_axes):
    return pltpu.CompilerParams(dimension_semantics=("arbitrary",) * n_axes,
                                vmem_limit_bytes=VMEM_LIMIT)


def _dot(a, b):
    return jnp.dot(a, b, preferred_element_type=F32)


def _dot_nt(a, b):
    return lax.dot_general(a, b, NT_DIMS, preferred_element_type=F32)


def _dot_tn(a, b):
    return lax.dot_general(a, b, TN_DIMS, preferred_element_type=F32)


def _rms(x, w):
    return x * lax.rsqrt(jnp.mean(x * x, axis=-1, keepdims=True) + NORM_EPS) * w


def _softplus(x):
    return jnp.maximum(x, 0.0) + jnp.log(1.0 + jnp.exp(-jnp.abs(x)))


def _cumsum_rows(x, rev):
    n = x.shape[0]
    row = lax.broadcasted_iota(I32, x.shape, 0)
    k = 1
    while k < n:
        if rev:
            x = x + jnp.where(row < n - k, pltpu.roll(x, n - k, 0), 0.0)
        else:
            x = x + jnp.where(row >= k, pltpu.roll(x, k, 0), 0.0)
        k *= 2
    return x


def _inproj_kernel(x_ref, nw_ref, w_ref, ws_ref, o_ref, os_ref, u_scr):
    @pl.when(pl.program_id(1) == 0)
    def _():
        ub = _rms(x_ref[...], nw_ref[...]).astype(BF16)
        u_scr[...] = ub
        os_ref[...] = _dot(ub, ws_ref[...])

    o_ref[...] = _dot(u_scr[...], w_ref[...])


def _in_projection(h, nw, w_main, w_small):
    t, d = h.shape
    n = w_main.shape[1]
    tm, tn = _tile(t, 1024), _tile(n, 1024)
    return pl.pallas_call(
        _inproj_kernel,
        grid=(t // tm, n // tn),
        in_specs=[pl.BlockSpec((tm, d), lambda i, j: (i, 0)),
                  pl.BlockSpec((1, d), lambda i, j: (0, 0)),
                  pl.BlockSpec((d, tn), lambda i, j: (0, j)),
                  pl.BlockSpec((d, LANES), lambda i, j: (0, 0))],
        out_specs=[pl.BlockSpec((tm, tn), lambda i, j: (i, j)),
                   pl.BlockSpec((tm, LANES), lambda i, j: (i, 0))],
        out_shape=[jax.ShapeDtypeStruct((t, n), F32), jax.ShapeDtypeStruct((t, LANES), F32)],
        scratch_shapes=[pltpu.VMEM((tm, d), BF16)],
        compiler_params=_params(2),
        name="in_projection",
    )(h, nw, w_main, w_small)


def _conv_kernel(prev_ref, x_ref, next_ref, w_ref, o_ref, ext_scr, *, rows, nblk, taps, hd):
    i = pl.program_id(1)
    g = pl.program_id(2)
    pad = SUBLANES
    ext_scr[0:pad, :] = jnp.where(i > 0, prev_ref[...], 0.0)
    ext_scr[pad:pad + rows, :] = x_ref[...]
    ext_scr[pad + rows:, :] = jnp.where(i < nblk - 1, next_ref[...], 0.0)
    first = pad - (taps - 1) // 2
    acc = ext_scr[pl.ds(first, rows), :] * w_ref[0:1, :]
    for j in range(1, taps):
        acc = acc + ext_scr[pl.ds(first + j, rows), :] * w_ref[j:j + 1, :]
    y = acc * jax.nn.sigmoid(acc)
    qscale = jnp.where(g == 0, hd ** -0.5, 1.0)
    for hh in range(y.shape[1] // hd):
        sl = slice(hh * hd, (hh + 1) * hd)
        yh = y[:, sl]
        r = lax.rsqrt(jnp.sum(yh * yh, axis=-1, keepdims=True) + NORM_EPS) * qscale
        o_ref[:, sl] = jnp.where(g == 2, yh, yh * r)


def _gdn_conv(proj3, conv_w, gw, hd):
    b, s, _ = proj3.shape
    taps = conv_w.shape[0]
    rows = _tile(s, 256)
    nblk = s // rows
    rb = rows // SUBLANES
    wpad = jnp.zeros((SUBLANES, 3 * gw), F32).at[:taps].set(conv_w)
    kern = functools.partial(_conv_kernel, rows=rows, nblk=nblk, taps=taps, hd=hd)
    return pl.pallas_call(
        kern,
        grid=(b, nblk, 3),
        in_specs=[pl.BlockSpec((None, SUBLANES, gw), lambda bb, i, g: (bb, jnp.maximum(i * rb - 1, 0), g)),
                  pl.BlockSpec((None, rows, gw), lambda bb, i, g: (bb, i, g)),
                  pl.BlockSpec((None, SUBLANES, gw),
                               lambda bb, i, g: (bb, jnp.minimum((i + 1) * rb, s // SUBLANES - 1), g)),
                  pl.BlockSpec((SUBLANES, gw), lambda bb, i, g: (0, g))],
        out_specs=pl.BlockSpec((None, rows, gw), lambda bb, i, g: (bb, i, g)),
        out_shape=jax.ShapeDtypeStruct((b, s, 3 * gw), F32),
        scratch_shapes=[pltpu.VMEM((rows + 2 * SUBLANES, gw), F32)],
        compiler_params=_params(3),
        name="gdn_conv",
    )(proj3, proj3, proj3, wpad)


def _head_group(heads):
    return 2 if heads % 2 == 0 else 1


def _pick(parts, block_id):
    out = parts[0]
    for i in range(1, len(parts)):
        out = jnp.where(block_id == i, parts[i], out)
    return out


def _gdn_kernel(q_ref, k_ref, v_ref, gs_ref, alog_ref, dt_ref, o_ref, s_scr, *, heads, hd, rev, d, nsub):
    c = CHUNK
    grp = _head_group(heads)
    r = grp * c
    groups = [list(range(g * grp, (g + 1) * grp)) for g in range(heads // grp)]
    units = [(sc, g) for sc in range(nsub) for g in range(len(groups))]
    shift = c.bit_length() - 1

    @pl.when(pl.program_id(1) == 0)
    def _():
        s_scr[...] = jnp.zeros_like(s_scr)

    gs = gs_ref[...]
    gall = -jnp.exp(alog_ref[...]) * _softplus(gs + dt_ref[...])
    bt_all = jax.nn.sigmoid(gs)
    row_blk = lax.broadcasted_iota(I32, (r, 1), 0) >> shift
    lane_blk = lax.broadcasted_iota(I32, (1, r), 1) >> shift
    ti = lax.broadcasted_iota(I32, (r, r), 0)
    si = lax.broadcasted_iota(I32, (r, r), 1)
    same = (ti >> shift) == (si >> shift)
    tl = ti & (c - 1)
    sl = si & (c - 1)
    incl = same & ((tl <= sl) if rev else (tl >= sl))
    strict = same & ((tl < sl) if rev else (tl > sl))
    eye = (ti == si).astype(F32)
    last = 0 if rev else c - 1
    gcol = lambda h: 2 * heads + d * heads + h
    gc1s, gcs, gcts, bts = [], [], [], []
    for sc in range(nsub):
        gc1 = _cumsum_rows(gall[sc * c:(sc + 1) * c], rev)
        gc = jnp.concatenate([gc1] * grp, axis=0)
        gc1s.append(gc1)
        gcs.append(gc)
        gcts.append(gc.T)
        bts.append(jnp.concatenate([bt_all[sc * c:(sc + 1) * c]] * grp, axis=0))

    def stack(ref, sc, hs):
        return jnp.concatenate([ref[sc * c:(sc + 1) * c, h * hd:(h + 1) * hd] for h in hs], axis=0)

    qs, ks, vs, gls, ys, ps, qks = [], [], [], [], [], [], []
    for sc, g in units:
        hs = groups[g]
        gc, gct, bt = gcs[sc], gcts[sc], bts[sc]
        gcc = _pick([gc[:, gcol(h):gcol(h) + 1] for h in hs], row_blk)
        gcr = _pick([gct[gcol(h):gcol(h) + 1, :] for h in hs], lane_blk)
        beta = _pick([bt[:, d * heads + h:d * heads + h + 1] for h in hs], row_blk)
        gl = _pick([gc1s[sc][last:last + 1, gcol(h):gcol(h) + 1] for h in hs], row_blk)
        q, k, v = stack(q_ref, sc, hs), stack(k_ref, sc, hs), stack(v_ref, sc, hs)
        decay = jnp.where(incl, jnp.exp(jnp.minimum(gcc - gcr, 0.0)), 0.0)
        eg = jnp.exp(gcc)
        kb = k * beta
        kbf = k.astype(BF16)
        y = -jnp.where(strict, _dot_nt(kb.astype(BF16), kbf) * decay, 0.0)
        qks.append((_dot_nt(q.astype(BF16), kbf) * decay).astype(BF16))
        qs.append(q * eg)
        ks.append(k * jnp.exp(gl - gcc))
        vs.append(jnp.concatenate([v * beta, kb * eg], axis=1).astype(BF16))
        gls.append(gl)
        ys.append(y)
        ps.append(eye + y)

    span = 2
    while span < c:
        ybs = [y.astype(BF16) for y in ys]
        ys = [_dot(yb, yb) for yb in ybs]
        ps = [p + _dot(p.astype(BF16), y.astype(BF16)) for p, y in zip(ps, ys)]
        span *= 2

    uws = [_dot(p.astype(BF16), rhs) for p, rhs in zip(ps, vs)]

    states = [s_scr[h] for h in range(heads)]
    for sc in (range(nsub - 1, -1, -1) if rev else range(nsub)):
        wss = {}
        for g, hs in enumerate(groups):
            un = sc * len(groups) + g
            for i, h in enumerate(hs):
                rows = slice(i * c, (i + 1) * c)
                lhs = jnp.concatenate([uws[un][rows, hd:], qs[un][rows, :]], axis=0).astype(BF16)
                wss[h] = _dot(lhs, states[h].astype(BF16))
        for g, hs in enumerate(groups):
            un = sc * len(groups) + g
            vnb = jnp.concatenate([uws[un][i * c:(i + 1) * c, :hd] - wss[h][:c] for i, h in enumerate(hs)],
                                  axis=0).astype(BF16)
            intra = _dot(qks[un], vnb)
            for i, h in enumerate(hs):
                rows = slice(i * c, (i + 1) * c)
                o_ref[sc * c:(sc + 1) * c, h * hd:(h + 1) * hd] = wss[h][c:] + intra[rows]
                states[h] = jnp.exp(gls[un][i * c:i * c + 1]) * states[h] + _dot_tn(
                    ks[un][rows].astype(BF16), vnb[rows])
    for h in range(heads):
        s_scr[h] = states[h]


def _scan_chunks_per_step(n_chunks):
    return 4 if n_chunks % 4 == 0 else 1


def _gdn_scan(qkv, gs3, alog_row, dt_row, heads, hd, d):
    b, s, _ = qkv.shape
    gw = heads * hd
    nsub = _scan_chunks_per_step(s // CHUNK)
    rows = nsub * CHUNK
    nblk = s // rows
    rev = d == 1
    cidx = (lambda cc: nblk - 1 - cc) if rev else (lambda cc: cc)
    kern = functools.partial(_gdn_kernel, heads=heads, hd=hd, rev=rev, d=d, nsub=nsub)
    col = lambda g: pl.BlockSpec((None, rows, gw), lambda bb, cc: (bb, cidx(cc), g))
    row = pl.BlockSpec((1, LANES), lambda bb, cc: (0, 0))
    return pl.pallas_call(
        kern,
        grid=(b, nblk),
        in_specs=[col(0), col(1), col(2),
                  pl.BlockSpec((None, rows, LANES), lambda bb, cc: (bb, cidx(cc), 0)), row, row],
        out_specs=pl.BlockSpec((None, rows, gw), lambda bb, cc: (bb, cidx(cc), 0)),
        out_shape=jax.ShapeDtypeStruct((b, s, gw), F32),
        scratch_shapes=[pltpu.VMEM((heads, hd, hd), F32)],
        compiler_params=_params(2),
        name="gdn_scan_bwd" if rev else "gdn_scan_fwd",
    )(qkv, qkv, qkv, gs3, alog_row, dt_row)


def _hgrn_kernel(q_ref, f_ref, v_ref, lbraw_ref, o_ref, s_scr, *, heads, hd, rev, layer, nsub):
    c = CHUNK
    grp = _head_group(heads)
    r = grp * c
    groups = [list(range(g * grp, (g + 1) * grp)) for g in range(heads // grp)]
    units = [(sc, g) for sc in range(nsub) for g in range(len(groups))]

    @pl.when(pl.program_id(1) == 0)
    def _():
        s_scr[...] = jnp.zeros_like(s_scr)

    ti = lax.broadcasted_iota(I32, (r, r), 0)
    si = lax.broadcasted_iota(I32, (r, r), 1)
    row = lax.broadcasted_iota(I32, (r, hd), 0)
    raw = lbraw_ref[...]
    ex = jnp.exp(raw - jnp.max(raw, axis=0, keepdims=True))
    share = ex / jnp.sum(ex, axis=0, keepdims=True)
    lb_all = jnp.zeros((1, share.shape[1]), F32)
    for j in range(1, layer + 1):
        lb_all = lb_all + share[j:j + 1, :]
    tiny = jnp.finfo(F32).tiny

    def stack(ref, sc, hs):
        return jnp.concatenate([ref[sc * c:(sc + 1) * c, h * hd:(h + 1) * hd] for h in hs], axis=0)

    qs, ks, vbs, ips, ess, tots, attns = [], [], [], [], [], [], []
    for sc, g in units:
        hs = groups[g]
        q, fpre = stack(q_ref, sc, hs), stack(f_ref, sc, hs)
        lb = jnp.concatenate([jnp.broadcast_to(lb_all[:, h * hd:(h + 1) * hd], (c, hd)) for h in hs], axis=0)
        f = jnp.maximum(lb + (1.0 - lb) * jax.nn.sigmoid(fpre), tiny)
        k = (1.0 - lb) * jax.nn.sigmoid(-fpre)
        qs.append(q)
        ks.append(k)
        vbs.append(stack(v_ref, sc, hs).astype(BF16))
        ips.append(f)
        ess.append(jnp.ones_like(f))
        tots.append(f)
        attns.append(jnp.where(ti == si, _dot_nt(q.astype(BF16), k.astype(BF16)), 0.0))
    m, shift = 1, 0
    while m < c:
        tb = ti >> shift
        sb = si >> shift
        if rev:
            pair = ((tb & 1) == 0) & (sb == tb + 1)
        else:
            pair = ((tb & 1) == 1) & (sb == tb - 1)
        later = (lambda blk: blk % 2 == 0) if rev else (lambda blk: blk % 2 == 1)
        for un in range(len(units)):
            lvl = _dot_nt((qs[un] * ips[un]).astype(BF16), (ks[un] * ess[un]).astype(BF16))
            if m < SUBLANES:
                attns[un] = attns[un] + jnp.where(pair, lvl, 0.0)
            else:
                attns[un] = jnp.concatenate(
                    [attns[un][b0:b0 + m] + jnp.where(pair[b0:b0 + m], lvl[b0:b0 + m], 0.0)
                     if later(b0 // m) else attns[un][b0:b0 + m] for b0 in range(0, r, m)], axis=0)
        for un in range(len(units)):
            ip, es, tot = ips[un], ess[un], tots[un]
            if m < SUBLANES:
                upper = ((row >> shift) & 1) == 1
                below = pltpu.roll(tot, m, 0)
                above = pltpu.roll(tot, r - m, 0)
                if rev:
                    ips[un] = ip * jnp.where(upper, 1.0, above)
                    ess[un] = es * jnp.where(upper, below, 1.0)
                else:
                    ips[un] = ip * jnp.where(upper, below, 1.0)
                    ess[un] = es * jnp.where(upper, 1.0, above)
                tots[un] = tot * jnp.where(upper, below, above)
            else:
                blocks = [(b0, (b0 // m ^ 1) * m) for b0 in range(0, r, m)]
                ips[un] = jnp.concatenate([ip[b0:b0 + m] * tot[s0:s0 + m] if later(b0 // m) else ip[b0:b0 + m]
                                           for b0, s0 in blocks], axis=0)
                ess[un] = jnp.concatenate([es[b0:b0 + m] if later(b0 // m) else es[b0:b0 + m] * tot[s0:s0 + m]
                                           for b0, s0 in blocks], axis=0)
                tots[un] = jnp.concatenate([tot[b0:b0 + m] * tot[s0:s0 + m] for b0, s0 in blocks], axis=0)
        m, shift = 2 * m, shift + 1
    intras = [_dot(attns[un].astype(BF16), vbs[un]) for un in range(len(units))]
    qis = [(qs[un] * ips[un]).astype(BF16) for un in range(len(units))]
    kes = [(ks[un] * ess[un]).astype(BF16) for un in range(len(units))]
    states = [s_scr[h] for h in range(heads)]
    for sc in (range(nsub - 1, -1, -1) if rev else range(nsub)):
        for g, hs in enumerate(groups):
            un = sc * len(groups) + g
            for i, h in enumerate(hs):
                rows = slice(i * c, (i + 1) * c)
                o_ref[sc * c:(sc + 1) * c, h * hd:(h + 1) * hd] = _dot_nt(
                    qis[un][rows], states[h].astype(BF16)) + intras[un][rows]
                states[h] = tots[un][i * c:i * c + 1, :] * states[h] + _dot_tn(vbs[un][rows], kes[un][rows])
    for h in range(heads):
        s_scr[h] = states[h]


def _hgrn_scan(proj3, lbraw, heads, hd, d, layer, qcol, fcol, vcol):
    b, s, _ = proj3.shape
    hw = heads * hd
    nsub = _scan_chunks_per_step(s // CHUNK)
    rows = nsub * CHUNK
    nblk = s // rows
    rev = d == 1
    depth = lbraw.shape[1]
    cidx = (lambda cc: nblk - 1 - cc) if rev else (lambda cc: cc)
    kern = functools.partial(_hgrn_kernel, heads=heads, hd=hd, rev=rev, layer=layer, nsub=nsub)
    col = lambda g: pl.BlockSpec((None, rows, hw), lambda bb, cc: (bb, cidx(cc), g))
    return pl.pallas_call(
        kern,
        grid=(b, nblk),
        in_specs=[col(qcol), col(fcol), col(vcol),
                  pl.BlockSpec((None, depth, hw), lambda bb, cc: (d, 0, 0))],
        out_specs=pl.BlockSpec((None, rows, hw), lambda bb, cc: (bb, cidx(cc), 0)),
        out_shape=jax.ShapeDtypeStruct((b, s, hw), F32),
        scratch_shapes=[pltpu.VMEM((heads, hd, hd), F32)],
        compiler_params=_params(2),
        name="hgrn_scan_bwd" if rev else "hgrn_scan_fwd",
    )(proj3, proj3, proj3, lbraw)


def _outproj_kernel(oaf_ref, oab_ref, z_ref, obf_ref, obb_ref, g_ref, h_ref, gnw_ref, hnw_ref, wout_ref,
                    nfw_ref, wr_ref, hn_ref, u_ref, p_ref, mix_scr, *, ha, hb, hd, n_exp):
    def gated(of_ref, ob_ref, gate_ref, w_ref, heads, base):
        for hh in range(heads):
            sl = slice(hh * hd, (hh + 1) * hd)
            o = of_ref[:, sl] + ob_ref[:, sl]
            zz = gate_ref[:, sl]
            y = _rms(o, w_ref[...]) * (zz * jax.nn.sigmoid(zz))
            mix_scr[:, base + hh * hd:base + (hh + 1) * hd] = y.astype(BF16)

    gated(oaf_ref, oab_ref, z_ref, gnw_ref, ha, 0)
    gated(obf_ref, obb_ref, g_ref, hnw_ref, hb, ha * hd)
    hn = h_ref[...] + _dot(mix_scr[...], wout_ref[...])
    hn_ref[...] = hn
    u = _rms(hn, nfw_ref[...])
    u_hi = u.astype(BF16)
    u_lo = (u - u_hi.astype(F32)).astype(BF16)
    hi_terms = _dot(u_hi, wr_ref[...])
    logits = hi_terms[:, :LANES] + hi_terms[:, LANES:] + _dot(u_lo, wr_ref[:, :LANES])
    lane = lax.broadcasted_iota(I32, logits.shape, 1)
    logits = jnp.where(lane < n_exp, logits, -1e30)
    ex = jnp.exp(logits - jnp.max(logits, axis=-1, keepdims=True))
    ex = jnp.where(lane < n_exp, ex, 0.0)
    probs = ex / jnp.sum(ex, axis=-1, keepdims=True)
    p_ref[...] = probs
    data_rows = u.shape[1] // LANES
    slab = data_rows + SUBLANES
    for cc in range(data_rows):
        u_ref[pl.ds(cc, u.shape[0], stride=slab), :] = u[:, cc * LANES:(cc + 1) * LANES]
    for cc in range(data_rows, slab):
        u_ref[pl.ds(cc, u.shape[0], stride=slab), :] = probs


def _slab_rows(d):
    return d // LANES + SUBLANES


def _out_projection(oaf, oab, obf, obb, proj, h, gnw, hnw, wout, nfw, wr_pad, ha, hb, hd, zcol, gcol, n_exp):
    t, d = h.shape
    gw, hw = ha * hd, hb * hd
    tm = _tile(t, 256)
    kern = functools.partial(_outproj_kernel, ha=ha, hb=hb, hd=hd, n_exp=n_exp)
    rowblk = lambda w: pl.BlockSpec((tm, w), lambda i: (i, 0))
    full = lambda shp: pl.BlockSpec(shp, lambda i: (0, 0))
    return pl.pallas_call(
        kern,
        grid=(t // tm,),
        in_specs=[rowblk(gw), rowblk(gw), pl.BlockSpec((tm, gw), lambda i: (i, zcol)),
                  rowblk(hw), rowblk(hw), pl.BlockSpec((tm, hw), lambda i: (i, gcol)),
                  rowblk(d), full((1, hd)), full((1, hd)), full((gw + hw, d)), full((1, d)),
                  full((d, 2 * LANES))],
        out_specs=[rowblk(d), pl.BlockSpec((tm * _slab_rows(d), LANES), lambda i: (i, 0)), rowblk(LANES)],
        out_shape=[jax.ShapeDtypeStruct((t, d), F32), jax.ShapeDtypeStruct((t * _slab_rows(d), LANES), F32),
                   jax.ShapeDtypeStruct((t, LANES), F32)],
        scratch_shapes=[pltpu.VMEM((tm, gw + hw), BF16)],
        compiler_params=_params(1),
        name="out_projection_router",
    )(oaf, oab, proj, obf, obb, proj, h, gnw, hnw, wout, nfw, wr_pad)


def _index_row_width(cap):
    return -(-cap // SMEM_BLOCK) * SMEM_BLOCK


def _topk_kernel(p_ref, sp_ref, incl_ref, pst_ref, *, s, cap):
    rb = _tile(s, 512)

    def count(pred):
        def body(i, acc):
            bits = pltpu.bitcast(p_ref[pl.ds(pl.multiple_of(i * rb, rb), rb), :], I32)
            return acc + jnp.sum(pred(bits).astype(I32).reshape(rb // SUBLANES, SUBLANES, LANES), axis=0)

        acc = lax.fori_loop(0, s // rb, body, jnp.zeros((SUBLANES, LANES), I32))
        return jnp.sum(acc, axis=0, keepdims=True)

    def bit_body(j, thr):
        cand = thr | jnp.left_shift(jnp.int32(1), 30 - j)
        return jnp.where(count(lambda bits: bits >= cand) >= cap, cand, thr)

    thr = lax.fori_loop(0, 31, bit_body, jnp.zeros((1, LANES), I32))
    need = (cap - count(lambda bits: bits > thr)).astype(F32)
    ri = lax.broadcasted_iota(I32, (LANES, LANES), 0)
    ci = lax.broadcasted_iota(I32, (LANES, LANES), 1)
    before = (ci < ri).astype(BF16)

    def chunk_body(cc, carry):
        cg, ce = carry
        r0 = pl.multiple_of(cc * LANES, LANES)
        bits = pltpu.bitcast(p_ref[pl.ds(r0, LANES), :], I32)
        gt = (bits > thr).astype(F32)
        eq = (bits == thr).astype(F32)
        pg = cg + _dot(before, gt.astype(BF16))
        pe = ce + _dot(before, eq.astype(BF16))
        sel = (gt > 0.0) | ((eq > 0.0) & (pe < need))
        pos = pg + jnp.minimum(pe, need)
        sp_ref[pl.ds(r0, LANES), :] = jnp.where(sel, pos, -1.0).astype(I32)
        incl_ref[pl.ds(r0, LANES), :] = pos + jnp.where(sel, 1.0, 0.0)
        pst_ref[cc] = jnp.broadcast_to(cg + jnp.minimum(ce, need), (SUBLANES, LANES)).astype(I32)
        return cg + jnp.sum(gt, axis=0, keepdims=True), ce + jnp.sum(eq, axis=0, keepdims=True)

    zero = jnp.zeros((1, LANES), F32)
    lax.fori_loop(0, s // LANES, chunk_body, (zero, zero))


def _topk_positions(probs3, cap):
    b, s, _ = probs3.shape
    kern = functools.partial(_topk_kernel, s=s, cap=cap)
    tok = pl.BlockSpec((None, s, LANES), lambda bb: (bb, 0, 0))
    return pl.pallas_call(
        kern,
        grid=(b,),
        in_specs=[tok],
        out_specs=[tok, tok, pl.BlockSpec((None, s // LANES, SUBLANES, LANES), lambda bb: (bb, 0, 0, 0))],
        out_shape=[jax.ShapeDtypeStruct((b, s, LANES), I32), jax.ShapeDtypeStruct((b, s, LANES), F32),
                   jax.ShapeDtypeStruct((b, s // LANES, SUBLANES, LANES), I32)],
        compiler_params=_params(1),
        name="expert_choice_topk",
    )(probs3)


def _slots_kernel(pst_ref, incl_ref, idx_ref, bc_scr, *, s, cap, n_exp):
    rb = _tile(s, 512)
    n_chunk = s // LANES
    bb = pl.program_id(0)
    idx_ref[...] = jnp.zeros_like(idx_ref)
    first_row = bb * s
    lane_f = lax.broadcasted_iota(I32, (1, LANES), 1).astype(F32)
    for e in range(n_exp):
        def fill(i, carry):
            r0 = pl.multiple_of(i * rb, rb)
            bc_scr[pl.ds(r0, rb), :] = jnp.broadcast_to(incl_ref[pl.ds(r0, rb), e:e + 1], (rb, LANES))
            return carry

        lax.fori_loop(0, s // rb, fill, 0)

        def chosen_before(chunk):
            return pst_ref[(bb * n_chunk + jnp.minimum(chunk, n_chunk - 1)) * n_exp + e]

        def slot_tile(jt, carry):
            full, reach = carry
            lo = jt * LANES
            full = lax.while_loop(lambda t: (t < n_chunk - 1) & (chosen_before(t + 1) <= lo), lambda t: t + 1, full)
            reach = lax.while_loop(lambda t: (t < n_chunk) & (chosen_before(t) <= lo + LANES - 1),
                                   lambda t: t + 1, reach)
            slot = lane_f + lax.convert_element_type(lo, F32)

            def body(t, acc):
                blk = bc_scr[pl.ds(pl.multiple_of(t * LANES, LANES), LANES), :]
                return acc + jnp.sum(jnp.where(blk <= slot, 1.0, 0.0).reshape(-1, COUNT_ROWS, LANES), axis=0)

            acc = lax.fori_loop(full, reach, body, jnp.zeros((COUNT_ROWS, LANES), F32))
            token = jnp.sum(acc, axis=0, keepdims=True).astype(I32) + full * LANES
            idx_ref[e:e + 1, pl.ds(pl.multiple_of(lo, LANES), LANES)] = token + first_row
            return full, reach

        lax.fori_loop(0, cap // LANES, slot_tile, (jnp.int32(0), jnp.int32(0)))


def _slot_tokens(pst_flat, incl, cap, n_exp):
    b, s, _ = incl.shape
    width = _index_row_width(cap)
    kern = functools.partial(_slots_kernel, s=s, cap=cap, n_exp=n_exp)
    grid_spec = pltpu.PrefetchScalarGridSpec(
        num_scalar_prefetch=1,
        grid=(b,),
        in_specs=[pl.BlockSpec((None, s, LANES), lambda bb, pst: (bb, 0, 0))],
        out_specs=pl.BlockSpec((None, n_exp, width), lambda bb, pst: (bb, 0, 0)),
        scratch_shapes=[pltpu.VMEM((s, LANES), F32)],
    )
    return pl.pallas_call(
        kern,
        grid_spec=grid_spec,
        out_shape=jax.ShapeDtypeStruct((b, n_exp, width), I32),
        compiler_params=_params(1),
        name="expert_choice_slots",
    )(pst_flat, incl)


def _row_copy(u_hbm, xs_scr, sem, src_row, dst_row, slab):
    return pltpu.make_async_copy(u_hbm.at[pl.ds(src_row * slab, slab)], xs_scr.at[pl.ds(dst_row * slab, slab)], sem)


def _expert_kernel(idx_ref, idx_next_ref, u_hbm, wg_ref, wu_ref, wd_ref, y_ref, xs_scr, xb_scr, gate_scr, acc_scr,
                   sem, *, ts, slab):
    e = pl.program_id(1)
    j = pl.program_id(2)
    fk = pl.program_id(3)
    n_j = pl.num_programs(2)
    tile = (pl.program_id(0) * pl.num_programs(1) + e) * n_j + j
    n_tiles = pl.num_programs(0) * pl.num_programs(1) * n_j
    data_rows = slab - SUBLANES

    def issue(ref, base):
        def body(r, carry):
            _row_copy(u_hbm, xs_scr, sem, ref[base + r], r, slab).start()
            return carry

        lax.fori_loop(0, ts, body, 0, unroll=8)

    @pl.when(fk == 0)
    def _():
        @pl.when(tile == 0)
        def _():
            issue(idx_ref, 0)

        def drain(r, carry):
            _row_copy(u_hbm, xs_scr, sem, 0, r, slab).wait()
            return carry

        lax.fori_loop(0, ts, drain, 0, unroll=8)
        for cc in range(data_rows):
            xb_scr[:, cc * LANES:(cc + 1) * LANES] = xs_scr[pl.ds(cc, ts, stride=slab), :].astype(BF16)
        probs = xs_scr[pl.ds(data_rows, ts, stride=slab), :]
        lane = lax.broadcasted_iota(I32, probs.shape, 1)
        gate_scr[...] = jnp.sum(jnp.where(lane == e, probs, 0.0), axis=1, keepdims=True)
        acc_scr[...] = jnp.zeros_like(acc_scr)

        @pl.when(j + 1 < n_j)
        def _():
            issue(idx_ref, (j + 1) * ts)

        @pl.when((j + 1 == n_j) & (tile + 1 < n_tiles))
        def _():
            issue(idx_next_ref, 0)

    xs = xb_scr[...]
    gate = _dot(xs, wg_ref[...])
    hid = (gate * jax.nn.sigmoid(gate) * _dot(xs, wu_ref[...])).astype(BF16)
    acc_scr[...] += _dot(hid, wd_ref[...])

    @pl.when(fk == pl.num_programs(3) - 1)
    def _():
        y_ref[...] = (acc_scr[...] * gate_scr[...]).astype(y_ref.dtype)


def _experts(idx_flat, u_slabs, wg, wu, wd, b, cap):
    n_exp, d, f = wg.shape
    slab = _slab_rows(d)
    ts, tf = _tile(cap, 512), _tile(f, 1024)
    width = _index_row_width(cap)
    last = b * n_exp - 1
    kern = functools.partial(_expert_kernel, ts=ts, slab=slab)
    return pl.pallas_call(
        kern,
        grid=(b, n_exp, cap // ts, f // tf),
        in_specs=[pl.BlockSpec((width,), lambda bb, e, j, k: (bb * n_exp + e,), memory_space=pltpu.SMEM),
                  pl.BlockSpec((width,), lambda bb, e, j, k: (jnp.minimum(bb * n_exp + e + 1, last),),
                               memory_space=pltpu.SMEM),
                  pl.BlockSpec(memory_space=pl.ANY),
                  pl.BlockSpec((None, d, tf), lambda bb, e, j, k: (e, 0, k)),
                  pl.BlockSpec((None, d, tf), lambda bb, e, j, k: (e, 0, k)),
                  pl.BlockSpec((None, tf, d), lambda bb, e, j, k: (e, k, 0))],
        out_specs=pl.BlockSpec((None, None, ts, d), lambda bb, e, j, k: (bb, e, j, 0)),
        out_shape=jax.ShapeDtypeStruct((b, n_exp, cap, d), BF16),
        scratch_shapes=[pltpu.VMEM((ts * slab, LANES), F32), pltpu.VMEM((ts, d), BF16), pltpu.VMEM((ts, 1), F32),
                        pltpu.VMEM((ts, d), F32), pltpu.SemaphoreType.DMA(())],
        compiler_params=_params(4),
        name="expert_ffn",
    )(idx_flat, idx_flat, u_slabs, wg, wu, wd)


def _combine_kernel(pst_ref, h_ref, sp_ref, y_hbm, nw_ref, o_ref, buf, sem, *, n_exp, cap, window, final):
    n_tiles = pl.num_programs(1)
    step = pl.program_id(0) * n_tiles + pl.program_id(1)
    n_steps = pl.num_programs(0) * n_tiles

    def window_start(at_step, e):
        first = pst_ref[at_step * n_exp + e] & -BF16_ROWS
        return pl.multiple_of(jnp.minimum(first, cap - window), BF16_ROWS)

    def window_copy(at_step, e):
        half = at_step % 2
        src = y_hbm.at[at_step // n_tiles, e, pl.ds(window_start(at_step, e), window)]
        return pltpu.make_async_copy(src, buf.at[half, pl.ds(e * window, window)], sem.at[half, e])

    @pl.when(step == 0)
    def _():
        for e in range(n_exp):
            window_copy(step, e).start()

    @pl.when(step + 1 < n_steps)
    def _():
        for e in range(n_exp):
            window_copy(step + 1, e).start()

    tm = h_ref.shape[0]
    lane = lax.broadcasted_iota(I32, (tm, LANES), 1)
    hits = [None] * (n_exp * window // LANES)
    for e in range(n_exp):
        window_copy(step, e).wait()
        slot = sp_ref[:, e:e + 1]
        target = jnp.where(slot >= 0, slot - window_start(step, e) + e * window, -1)
        for col in range(e * window // LANES, (e * window + window - 1) // LANES + 1):
            hit = target == lane + col * LANES
            hits[col] = hit if hits[col] is None else hits[col] | hit
    onehot = jnp.concatenate([jnp.where(hit, 1.0, 0.0).astype(BF16) for hit in hits], axis=1)
    acc = h_ref[...] + _dot(onehot, buf[step % 2])
    if final:
        acc = _rms(acc, nw_ref[...])
    o_ref[...] = acc


def _combine(pst_flat, h, sp, y, nw, n_exp, cap, final):
    b, s, d = h.shape
    tm = LANES
    window = tm + BF16_ROWS
    assert (n_exp * window) % LANES == 0
    kern = functools.partial(_combine_kernel, n_exp=n_exp, cap=cap, window=window, final=final)
    tok = lambda w: pl.BlockSpec((None, tm, w), lambda bb, i, pst: (bb, i, 0))
    grid_spec = pltpu.PrefetchScalarGridSpec(
        num_scalar_prefetch=1,
        grid=(b, s // tm),
        in_specs=[tok(d), tok(LANES), pl.BlockSpec(memory_space=pl.ANY),
                  pl.BlockSpec((1, d), lambda bb, i, pst: (0, 0))],
        out_specs=tok(d),
        scratch_shapes=[pltpu.VMEM((2, n_exp * window, d), BF16), pltpu.SemaphoreType.DMA((2, n_exp))],
    )
    return pl.pallas_call(
        kern,
        grid_spec=grid_spec,
        out_shape=jax.ShapeDtypeStruct((b, s, d), F32),
        compiler_params=_params(2),
        name="moe_combine",
    )(pst_flat, h, sp, y, nw)


def _lane_row(vals):
    return jnp.zeros((1, LANES), F32).at[0, :vals.shape[0]].set(vals.astype(F32))


def kernel(x, norm_mix, norm_ffn, norm_final, w_in, conv_w, gdn_a_log, gdn_dt_bias, gdn_norm, hgrn_lower_bounds,
           hgrn_norm, w_out, w_router, w_gate, w_up, w_down):
    b, s, d = x.shape
    depth = w_in.shape[0]
    ha = gdn_a_log.shape[-1]
    hd = gdn_norm.shape[-1]
    gw = ha * hd
    hw = hgrn_lower_bounds.shape[-1]
    hb = hw // hd
    n_exp = w_router.shape[-1]
    cap = CAPACITY_FACTOR * s // n_exp
    assert hd == LANES and gw == hw and s % LANES == 0 and 4 * ha <= LANES and n_exp <= LANES
    assert cap >= LANES + BF16_ROWS and cap % LANES == 0
    t = b * s
    sizes = (gw, gw, gw, gw, 2 * ha, 2 * ha, hw, 2 * hw, hw, hw)
    offs = [0]
    for sz in sizes:
        offs.append(offs[-1] + sz)
    lb_dir_major = jnp.transpose(hgrn_lower_bounds, (1, 0, 2))
    h = x.reshape(t, d)
    out = None
    for l in range(depth):
        wi = w_in[l]
        part = lambda n: wi[:, offs[n]:offs[n + 1]]
        w_main = jnp.concatenate([part(0), part(1), part(2), part(3), part(6), part(7), part(8), part(9)],
                                 axis=1).astype(BF16)
        w_small = jnp.zeros((d, LANES), F32).at[:, :4 * ha].set(
            jnp.concatenate([part(4), part(5)], axis=1)).astype(BF16)
        proj, gs = _in_projection(h, norm_mix[l][None, :], w_main, w_small)
        proj3 = proj.reshape(b, s, -1)
        gs3 = gs.reshape(b, s, LANES)
        qkv = _gdn_conv(proj3, conv_w[l], gw, hd)
        alog_row = jnp.zeros((1, LANES), F32).at[0, 2 * ha:4 * ha].set(gdn_a_log[l].reshape(-1))
        dt_row = jnp.zeros((1, LANES), F32).at[0, 2 * ha:4 * ha].set(gdn_dt_bias[l].reshape(-1))
        oa = [_gdn_scan(qkv, gs3, alog_row, dt_row, ha, hd, dd).reshape(t, gw) for dd in (0, 1)]
        ob = [_hgrn_scan(proj3, lb_dir_major, hb, hd, dd, l, 4, 5 + dd, 7).reshape(t, hw) for dd in (0, 1)]
        wr_f32 = jnp.zeros((d, LANES), F32).at[:, :n_exp].set(w_router[l])
        wr_hi = wr_f32.astype(BF16)
        wr_pad = jnp.concatenate([wr_hi, (wr_f32 - wr_hi.astype(F32)).astype(BF16)], axis=1)
        hn, u, probs = _out_projection(oa[0], oa[1], ob[0], ob[1], proj, h, gdn_norm[l][None, :],
                                       hgrn_norm[l][None, :], w_out[l].astype(BF16), norm_ffn[l][None, :],
                                       wr_pad, ha, hb, hd, 3, 8, n_exp)
        sp, incl, pst = _topk_positions(probs.reshape(b, s, LANES), cap)
        pst_flat = pst[:, :, 0, :n_exp].reshape(-1)
        idx = _slot_tokens(pst_flat, incl, cap, n_exp)
        y = _experts(idx.reshape(-1), u, w_gate[l].astype(BF16), w_up[l].astype(BF16), w_down[l].astype(BF16),
                     b, cap)
        out = _combine(pst_flat, hn.reshape(b, s, d), sp, y, norm_final[None, :], n_exp, cap, l == depth - 1)
        h = out.reshape(t, d)
    return out
```

```python
import functools

import jax
import jax.numpy as jnp
from jax import lax
from jax.experimental import pallas as pl
from jax.experimental.pallas import tpu as pltpu

F32, BF16, I32 = jnp.float32, jnp.bfloat16, jnp.int32
NORM_EPS = 1e-6
CHUNK = 64
CAPACITY_FACTOR = 2
LANES = 128
SUBLANES = 8
BF16_ROWS = 16
SMEM_BLOCK = 1024
COUNT_ROWS = 8 * SUBLANES
VMEM_LIMIT = 56 * 1024 * 1024
NT_DIMS = (((1,), (1,)), ((), ()))
TN_DIMS = (((0,), (0,)), ((), ()))


def _tile(n, pref):
    return pref if n % pref == 0 else n


def _params(n_axes):
    return pltpu.CompilerParams(dimension_semantics=("arbitrary",) * n_axes,
                                vmem_limit_bytes=VMEM_LIMIT)


def _dot(a, b):
    return jnp.dot(a, b, preferred_element_type=F32)


def _dot_nt(a, b):
    return lax.dot_general(a, b, NT_DIMS, preferred_element_type=F32)


def _dot_tn(a, b):
    return lax.dot_general(a, b, TN_DIMS, preferred_element_type=F32)


def _rms(x, w):
    return x * lax.rsqrt(jnp.mean(x * x, axis=-1, keepdims=True) + NORM_EPS) * w


def _softplus(x):
    return jnp.maximum(x, 0.0) + jnp.log(1.0 + jnp.exp(-jnp.abs(x)))


def _cumsum_rows(x, rev):
    n = x.shape[0]
    row = lax.broadcasted_iota(I32, x.shape, 0)
    k = 1
    while k < n:
        if rev:
            x = x + jnp.where(row < n - k, pltpu.roll(x, n - k, 0), 0.0)
        else:
            x = x + jnp.where(row >= k, pltpu.roll(x, k, 0), 0.0)
        k *= 2
    return x


def _inproj_kernel(x_ref, nw_ref, w_ref, ws_ref, o_ref, os_ref, u_scr):
    @pl.when(pl.program_id(1) == 0)
    def _():
        ub = _rms(x_ref[...], nw_ref[...]).astype(BF16)
        u_scr[...] = ub
        os_ref[...] = _dot(ub, ws_ref[...])

    o_ref[...] = _dot(u_scr[...], w_ref[...])


def _in_projection(h, nw, w_main, w_small):
    t, d = h.shape
    n = w_main.shape[1]
    tm, tn = _tile(t, 1024), _tile(n, 1024)
    return pl.pallas_call(
        _inproj_kernel,
        grid=(t // tm, n // tn),
        in_specs=[pl.BlockSpec((tm, d), lambda i, j: (i, 0)),
                  pl.BlockSpec((1, d), lambda i, j: (0, 0)),
                  pl.BlockSpec((d, tn), lambda i, j: (0, j)),
                  pl.BlockSpec((d, LANES), lambda i, j: (0, 0))],
        out_specs=[pl.BlockSpec((tm, tn), lambda i, j: (i, j)),
                   pl.BlockSpec((tm, LANES), lambda i, j: (i, 0))],
        out_shape=[jax.ShapeDtypeStruct((t, n), F32), jax.ShapeDtypeStruct((t, LANES), F32)],
        scratch_shapes=[pltpu.VMEM((tm, d), BF16)],
        compiler_params=_params(2),
        name="in_projection",
    )(h, nw, w_main, w_small)


def _conv_kernel(prev_ref, x_ref, next_ref, w_ref, o_ref, ext_scr, *, rows, nblk, taps, hd):
    i = pl.program_id(1)
    g = pl.program_id(2)
    pad = SUBLANES
    ext_scr[0:pad, :] = jnp.where(i > 0, prev_ref[...], 0.0)
    ext_scr[pad:pad + rows, :] = x_ref[...]
    ext_scr[pad + rows:, :] = jnp.where(i < nblk - 1, next_ref[...], 0.0)
    first = pad - (taps - 1) // 2
    acc = ext_scr[pl.ds(first, rows), :] * w_ref[0:1, :]
    for j in range(1, taps):
        acc = acc + ext_scr[pl.ds(first + j, rows), :] * w_ref[j:j + 1, :]
    y = acc * jax.nn.sigmoid(acc)
    qscale = jnp.where(g == 0, hd ** -0.5, 1.0)
    for hh in range(y.shape[1] // hd):
        sl = slice(hh * hd, (hh + 1) * hd)
        yh = y[:, sl]
        r = lax.rsqrt(jnp.sum(yh * yh, axis=-1, keepdims=True) + NORM_EPS) * qscale
        o_ref[:, sl] = jnp.where(g == 2, yh, yh * r)


def _gdn_conv(proj3, conv_w, gw, hd):
    b, s, _ = proj3.shape
    taps = conv_w.shape[0]
    rows = _tile(s, 512)
    nblk = s // rows
    rb = rows // SUBLANES
    wpad = jnp.zeros((SUBLANES, 3 * gw), F32).at[:taps].set(conv_w)
    kern = functools.partial(_conv_kernel, rows=rows, nblk=nblk, taps=taps, hd=hd)
    return pl.pallas_call(
        kern,
        grid=(b, nblk, 3),
        in_specs=[pl.BlockSpec((None, SUBLANES, gw), lambda bb, i, g: (bb, jnp.maximum(i * rb - 1, 0), g)),
                  pl.BlockSpec((None, rows, gw), lambda bb, i, g: (bb, i, g)),
                  pl.BlockSpec((None, SUBLANES, gw),
                               lambda bb, i, g: (bb, jnp.minimum((i + 1) * rb, s // SUBLANES - 1), g)),
                  pl.BlockSpec((SUBLANES, gw), lambda bb, i, g: (0, g))],
        out_specs=pl.BlockSpec((None, rows, gw), lambda bb, i, g: (bb, i, g)),
        out_shape=jax.ShapeDtypeStruct((b, s, 3 * gw), F32),
        scratch_shapes=[pltpu.VMEM((rows + 2 * SUBLANES, gw), F32)],
        compiler_params=_params(3),
        name="gdn_conv",
    )(proj3, proj3, proj3, wpad)


def _head_group(heads):
    return 2 if heads % 2 == 0 else 1


def _pick(parts, block_id):
    out = parts[0]
    for i in range(1, len(parts)):
        out = jnp.where(block_id == i, parts[i], out)
    return out


def _gdn_kernel(q_ref, k_ref, v_ref, gs_ref, alog_ref, dt_ref, o_ref, s_scr, *, heads, hd, rev, d, nsub):
    c = CHUNK
    grp = _head_group(heads)
    r = grp * c
    groups = [list(range(g * grp, (g + 1) * grp)) for g in range(heads // grp)]
    units = [(sc, g) for sc in range(nsub) for g in range(len(groups))]
    shift = c.bit_length() - 1

    @pl.when(pl.program_id(1) == 0)
    def _():
        s_scr[...] = jnp.zeros_like(s_scr)

    gs = gs_ref[...]
    gall = -jnp.exp(alog_ref[...]) * _softplus(gs + dt_ref[...])
    bt_all = jax.nn.sigmoid(gs)
    row_blk = lax.broadcasted_iota(I32, (r, 1), 0) >> shift
    lane_blk = lax.broadcasted_iota(I32, (1, r), 1) >> shift
    ti = lax.broadcasted_iota(I32, (r, r), 0)
    si = lax.broadcasted_iota(I32, (r, r), 1)
    same = (ti >> shift) == (si >> shift)
    tl = ti & (c - 1)
    sl = si & (c - 1)
    incl = same & ((tl <= sl) if rev else (tl >= sl))
    strict = same & ((tl < sl) if rev else (tl > sl))
    eye = (ti == si).astype(F32)
    last = 0 if rev else c - 1
    gcol = lambda h: 2 * heads + d * heads + h
    gc1s, gcs, gcts, bts = [], [], [], []
    for sc in range(nsub):
        gc1 = _cumsum_rows(gall[sc * c:(sc + 1) * c], rev)
        gc = jnp.concatenate([gc1] * grp, axis=0)
        gc1s.append(gc1)
        gcs.append(gc)
        gcts.append(gc.T)
        bts.append(jnp.concatenate([bt_all[sc * c:(sc + 1) * c]] * grp, axis=0))

    def stack(ref, sc, hs):
        return jnp.concatenate([ref[sc * c:(sc + 1) * c, h * hd:(h + 1) * hd] for h in hs], axis=0)

    qs, ks, vs, gls, ys, ps, qks = [], [], [], [], [], [], []
    for sc, g in units:
        hs = groups[g]
        gc, gct, bt = gcs[sc], gcts[sc], bts[sc]
        gcc = _pick([gc[:, gcol(h):gcol(h) + 1] for h in hs], row_blk)
        gcr = _pick([gct[gcol(h):gcol(h) + 1, :] for h in hs], lane_blk)
        beta = _pick([bt[:, d * heads + h:d * heads + h + 1] for h in hs], row_blk)
        gl = _pick([gc1s[sc][last:last + 1, gcol(h):gcol(h) + 1] for h in hs], row_blk)
        q, k, v = stack(q_ref, sc, hs), stack(k_ref, sc, hs), stack(v_ref, sc, hs)
        decay = jnp.where(incl, jnp.exp(jnp.minimum(gcc - gcr, 0.0)), 0.0)
        eg = jnp.exp(gcc)
        kb = k * beta
        kbf = k.astype(BF16)
        y = -jnp.where(strict, _dot_nt(kb.astype(BF16), kbf) * decay, 0.0)
        qks.append((_dot_nt(q.astype(BF16), kbf) * decay).astype(BF16))
        qs.append(q * eg)
        ks.append(k * jnp.exp(gl - gcc))
        vs.append(jnp.concatenate([v * beta, kb * eg], axis=1).astype(BF16))
        gls.append(gl)
        ys.append(y)
        ps.append(eye + y)

    span = 2
    while span < c:
        ybs = [y.astype(BF16) for y in ys]
        ys = [_dot(yb, yb) for yb in ybs]
        ps = [p + _dot(p.astype(BF16), y.astype(BF16)) for p, y in zip(ps, ys)]
        span *= 2

    uws = [_dot(p.astype(BF16), rhs) for p, rhs in zip(ps, vs)]

    states = [s_scr[h] for h in range(heads)]
    for sc in (range(nsub - 1, -1, -1) if rev else range(nsub)):
        wss = {}
        for g, hs in enumerate(groups):
            un = sc * len(groups) + g
            for i, h in enumerate(hs):
                rows = slice(i * c, (i + 1) * c)
                lhs = jnp.concatenate([uws[un][rows, hd:], qs[un][rows, :]], axis=0).astype(BF16)
                wss[h] = _dot(lhs, states[h].astype(BF16))
        for g, hs in enumerate(groups):
            un = sc * len(groups) + g
            vnb = jnp.concatenate([uws[un][i * c:(i + 1) * c, :hd] - wss[h][:c] for i, h in enumerate(hs)],
                                  axis=0).astype(BF16)
            intra = _dot(qks[un], vnb)
            for i, h in enumerate(hs):
                rows = slice(i * c, (i + 1) * c)
                o_ref[sc * c:(sc + 1) * c, h * hd:(h + 1) * hd] = wss[h][c:] + intra[rows]
                states[h] = jnp.exp(gls[un][i * c:i * c + 1]) * states[h] + _dot_tn(
                    ks[un][rows].astype(BF16), vnb[rows])
    for h in range(heads):
        s_scr[h] = states[h]


def _scan_chunks_per_step(n_chunks):
    return 4 if n_chunks % 4 == 0 else 1


def _gdn_scan(qkv, gs3, alog_row, dt_row, heads, hd, d):
    b, s, _ = qkv.shape
    gw = heads * hd
    nsub = _scan_chunks_per_step(s // CHUNK)
    rows = nsub * CHUNK
    nblk = s // rows
    rev = d == 1
    cidx = (lambda cc: nblk - 1 - cc) if rev else (lambda cc: cc)
    kern = functools.partial(_gdn_kernel, heads=heads, hd=hd, rev=rev, d=d, nsub=nsub)
    col = lambda g: pl.BlockSpec((None, rows, gw), lambda bb, cc: (bb, cidx(cc), g))
    row = pl.BlockSpec((1, LANES), lambda bb, cc: (0, 0))
    return pl.pallas_call(
        kern,
        grid=(b, nblk),
        in_specs=[col(0), col(1), col(2),
                  pl.BlockSpec((None, rows, LANES), lambda bb, cc: (bb, cidx(cc), 0)), row, row],
        out_specs=pl.BlockSpec((None, rows, gw), lambda bb, cc: (bb, cidx(cc), 0)),
        out_shape=jax.ShapeDtypeStruct((b, s, gw), F32),
        scratch_shapes=[pltpu.VMEM((heads, hd, hd), F32)],
        compiler_params=_params(2),
        name="gdn_scan_bwd" if rev else "gdn_scan_fwd",
    )(qkv, qkv, qkv, gs3, alog_row, dt_row)


def _hgrn_kernel(q_ref, f_ref, v_ref, lbraw_ref, o_ref, s_scr, *, heads, hd, rev, layer, nsub):
    c = CHUNK
    grp = _head_group(heads)
    r = grp * c
    groups = [list(range(g * grp, (g + 1) * grp)) for g in range(heads // grp)]
    units = [(sc, g) for sc in range(nsub) for g in range(len(groups))]

    @pl.when(pl.program_id(1) == 0)
    def _():
        s_scr[...] = jnp.zeros_like(s_scr)

    ti = lax.broadcasted_iota(I32, (r, r), 0)
    si = lax.broadcasted_iota(I32, (r, r), 1)
    row = lax.broadcasted_iota(I32, (r, hd), 0)
    raw = lbraw_ref[...]
    ex = jnp.exp(raw - jnp.max(raw, axis=0, keepdims=True))
    share = ex / jnp.sum(ex, axis=0, keepdims=True)
    lb_all = jnp.zeros((1, share.shape[1]), F32)
    for j in range(1, layer + 1):
        lb_all = lb_all + share[j:j + 1, :]
    tiny = jnp.finfo(F32).tiny

    def stack(ref, sc, hs):
        return jnp.concatenate([ref[sc * c:(sc + 1) * c, h * hd:(h + 1) * hd] for h in hs], axis=0)

    qs, ks, vbs, ips, ess, tots, attns = [], [], [], [], [], [], []
    for sc, g in units:
        hs = groups[g]
        q, fpre = stack(q_ref, sc, hs), stack(f_ref, sc, hs)
        lb = jnp.concatenate([jnp.broadcast_to(lb_all[:, h * hd:(h + 1) * hd], (c, hd)) for h in hs], axis=0)
        f = jnp.maximum(lb + (1.0 - lb) * jax.nn.sigmoid(fpre), tiny)
        k = (1.0 - lb) * jax.nn.sigmoid(-fpre)
        qs.append(q)
        ks.append(k)
        vbs.append(stack(v_ref, sc, hs).astype(BF16))
        ips.append(f)
        ess.append(jnp.ones_like(f))
        tots.append(f)
        attns.append(jnp.where(ti == si, _dot_nt(q.astype(BF16), k.astype(BF16)), 0.0))
    m, shift = 1, 0
    while m < c:
        tb = ti >> shift
        sb = si >> shift
        if rev:
            pair = ((tb & 1) == 0) & (sb == tb + 1)
        else:
            pair = ((tb & 1) == 1) & (sb == tb - 1)
        later = (lambda blk: blk % 2 == 0) if rev else (lambda blk: blk % 2 == 1)
        for un in range(len(units)):
            lvl = _dot_nt((qs[un] * ips[un]).astype(BF16), (ks[un] * ess[un]).astype(BF16))
            if m < SUBLANES:
                attns[un] = attns[un] + jnp.where(pair, lvl, 0.0)
            else:
                attns[un] = jnp.concatenate(
                    [attns[un][b0:b0 + m] + jnp.where(pair[b0:b0 + m], lvl[b0:b0 + m], 0.0)
                     if later(b0 // m) else attns[un][b0:b0 + m] for b0 in range(0, r, m)], axis=0)
        for un in range(len(units)):
            ip, es, tot = ips[un], ess[un], tots[un]
            if m < SUBLANES:
                upper = ((row >> shift) & 1) == 1
                below = pltpu.roll(tot, m, 0)
                above = pltpu.roll(tot, r - m, 0)
                if rev:
                    ips[un] = ip * jnp.where(upper, 1.0, above)
                    ess[un] = es * jnp.where(upper, below, 1.0)
                else:
                    ips[un] = ip * jnp.where(upper, below, 1.0)
                    ess[un] = es * jnp.where(upper, 1.0, above)
                tots[un] = tot * jnp.where(upper, below, above)
            else:
                blocks = [(b0, (b0 // m ^ 1) * m) for b0 in range(0, r, m)]
                ips[un] = jnp.concatenate([ip[b0:b0 + m] * tot[s0:s0 + m] if later(b0 // m) else ip[b0:b0 + m]
                                           for b0, s0 in blocks], axis=0)
                ess[un] = jnp.concatenate([es[b0:b0 + m] if later(b0 // m) else es[b0:b0 + m] * tot[s0:s0 + m]
                                           for b0, s0 in blocks], axis=0)
                tots[un] = jnp.concatenate([tot[b0:b0 + m] * tot[s0:s0 + m] for b0, s0 in blocks], axis=0)
        m, shift = 2 * m, shift + 1
    intras = [_dot(attns[un].astype(BF16), vbs[un]) for un in range(len(units))]
    qis = [(qs[un] * ips[un]).astype(BF16) for un in range(len(units))]
    kes = [(ks[un] * ess[un]).astype(BF16) for un in range(len(units))]
    states = [s_scr[h] for h in range(heads)]
    for sc in (range(nsub - 1, -1, -1) if rev else range(nsub)):
        for g, hs in enumerate(groups):
            un = sc * len(groups) + g
            for i, h in enumerate(hs):
                rows = slice(i * c, (i + 1) * c)
                o_ref[sc * c:(sc + 1) * c, h * hd:(h + 1) * hd] = _dot_nt(
                    qis[un][rows], states[h].astype(BF16)) + intras[un][rows]
                states[h] = tots[un][i * c:i * c + 1, :] * states[h] + _dot_tn(vbs[un][rows], kes[un][rows])
    for h in range(heads):
        s_scr[h] = states[h]


def _hgrn_scan(proj3, lbraw, heads, hd, d, layer, qcol, fcol, vcol):
    b, s, _ = proj3.shape
    hw = heads * hd
    nsub = _scan_chunks_per_step(s // CHUNK)
    rows = nsub * CHUNK
    nblk = s // rows
    rev = d == 1
    depth = lbraw.shape[1]
    cidx = (lambda cc: nblk - 1 - cc) if rev else (lambda cc: cc)
    kern = functools.partial(_hgrn_kernel, heads=heads, hd=hd, rev=rev, layer=layer, nsub=nsub)
    col = lambda g: pl.BlockSpec((None, rows, hw), lambda bb, cc: (bb, cidx(cc), g))
    return pl.pallas_call(
        kern,
        grid=(b, nblk),
        in_specs=[col(qcol), col(fcol), col(vcol),
                  pl.BlockSpec((None, depth, hw), lambda bb, cc: (d, 0, 0))],
        out_specs=pl.BlockSpec((None, rows, hw), lambda bb, cc: (bb, cidx(cc), 0)),
        out_shape=jax.ShapeDtypeStruct((b, s, hw), F32),
        scratch_shapes=[pltpu.VMEM((heads, hd, hd), F32)],
        compiler_params=_params(2),
        name="hgrn_scan_bwd" if rev else "hgrn_scan_fwd",
    )(proj3, proj3, proj3, lbraw)


def _outproj_kernel(oaf_ref, oab_ref, z_ref, obf_ref, obb_ref, g_ref, h_ref, gnw_ref, hnw_ref, wout_ref,
                    nfw_ref, wr_ref, hn_ref, u_ref, p_ref, mix_scr, *, ha, hb, hd, n_exp):
    def gated(of_ref, ob_ref, gate_ref, w_ref, heads, base):
        for hh in range(heads):
            sl = slice(hh * hd, (hh + 1) * hd)
            o = of_ref[:, sl] + ob_ref[:, sl]
            zz = gate_ref[:, sl]
            y = _rms(o, w_ref[...]) * (zz * jax.nn.sigmoid(zz))
            mix_scr[:, base + hh * hd:base + (hh + 1) * hd] = y.astype(BF16)

    gated(oaf_ref, oab_ref, z_ref, gnw_ref, ha, 0)
    gated(obf_ref, obb_ref, g_ref, hnw_ref, hb, ha * hd)
    hn = h_ref[...] + _dot(mix_scr[...], wout_ref[...])
    hn_ref[...] = hn
    u = _rms(hn, nfw_ref[...])
    u_hi = u.astype(BF16)
    u_lo = (u - u_hi.astype(F32)).astype(BF16)
    hi_terms = _dot(u_hi, wr_ref[...])
    logits = hi_terms[:, :LANES] + hi_terms[:, LANES:] + _dot(u_lo, wr_ref[:, :LANES])
    lane = lax.broadcasted_iota(I32, logits.shape, 1)
    logits = jnp.where(lane < n_exp, logits, -1e30)
    ex = jnp.exp(logits - jnp.max(logits, axis=-1, keepdims=True))
    ex = jnp.where(lane < n_exp, ex, 0.0)
    probs = ex / jnp.sum(ex, axis=-1, keepdims=True)
    p_ref[...] = probs
    data_rows = u.shape[1] // LANES
    slab = data_rows + SUBLANES
    for cc in range(data_rows):
        u_ref[pl.ds(cc, u.shape[0], stride=slab), :] = u[:, cc * LANES:(cc + 1) * LANES]
    for cc in range(data_rows, slab):
        u_ref[pl.ds(cc, u.shape[0], stride=slab), :] = probs


def _slab_rows(d):
    return d // LANES + SUBLANES


def _out_projection(oaf, oab, obf, obb, proj, h, gnw, hnw, wout, nfw, wr_pad, ha, hb, hd, zcol, gcol, n_exp):
    t, d = h.shape
    gw, hw = ha * hd, hb * hd
    tm = _tile(t, 256)
    kern = functools.partial(_outproj_kernel, ha=ha, hb=hb, hd=hd, n_exp=n_exp)
    rowblk = lambda w: pl.BlockSpec((tm, w), lambda i: (i, 0))
    full = lambda shp: pl.BlockSpec(shp, lambda i: (0, 0))
    return pl.pallas_call(
        kern,
        grid=(t // tm,),
        in_specs=[rowblk(gw), rowblk(gw), pl.BlockSpec((tm, gw), lambda i: (i, zcol)),
                  rowblk(hw), rowblk(hw), pl.BlockSpec((tm, hw), lambda i: (i, gcol)),
                  rowblk(d), full((1, hd)), full((1, hd)), full((gw + hw, d)), full((1, d)),
                  full((d, 2 * LANES))],
        out_specs=[rowblk(d), pl.BlockSpec((tm * _slab_rows(d), LANES), lambda i: (i, 0)), rowblk(LANES)],
        out_shape=[jax.ShapeDtypeStruct((t, d), F32), jax.ShapeDtypeStruct((t * _slab_rows(d), LANES), F32),
                   jax.ShapeDtypeStruct((t, LANES), F32)],
        scratch_shapes=[pltpu.VMEM((tm, gw + hw), BF16)],
        compiler_params=_params(1),
        name="out_projection_router",
    )(oaf, oab, proj, obf, obb, proj, h, gnw, hnw, wout, nfw, wr_pad)


def _index_row_width(cap):
    return -(-cap // SMEM_BLOCK) * SMEM_BLOCK


def _topk_kernel(p_ref, sp_ref, incl_ref, pst_ref, *, s, cap):
    rb = _tile(s, 512)

    def count(pred):
        def body(i, acc):
            bits = pltpu.bitcast(p_ref[pl.ds(pl.multiple_of(i * rb, rb), rb), :], I32)
            return acc + jnp.sum(pred(bits).astype(I32).reshape(rb // SUBLANES, SUBLANES, LANES), axis=0)

        acc = lax.fori_loop(0, s // rb, body, jnp.zeros((SUBLANES, LANES), I32))
        return jnp.sum(acc, axis=0, keepdims=True)

    def bit_body(j, thr):
        cand = thr | jnp.left_shift(jnp.int32(1), 30 - j)
        return jnp.where(count(lambda bits: bits >= cand) >= cap, cand, thr)

    thr = lax.fori_loop(0, 31, bit_body, jnp.zeros((1, LANES), I32))
    need = (cap - count(lambda bits: bits > thr)).astype(F32)
    ri = lax.broadcasted_iota(I32, (LANES, LANES), 0)
    ci = lax.broadcasted_iota(I32, (LANES, LANES), 1)
    before = (ci < ri).astype(BF16)

    def chunk_body(cc, carry):
        cg, ce = carry
        r0 = pl.multiple_of(cc * LANES, LANES)
        bits = pltpu.bitcast(p_ref[pl.ds(r0, LANES), :], I32)
        gt = (bits > thr).astype(F32)
        eq = (bits == thr).astype(F32)
        pg = cg + _dot(before, gt.astype(BF16))
        pe = ce + _dot(before, eq.astype(BF16))
        sel = (gt > 0.0) | ((eq > 0.0) & (pe < need))
        pos = pg + jnp.minimum(pe, need)
        sp_ref[pl.ds(r0, LANES), :] = jnp.where(sel, pos, -1.0).astype(I32)
        incl_ref[pl.ds(r0, LANES), :] = pos + jnp.where(sel, 1.0, 0.0)
        pst_ref[cc] = jnp.broadcast_to(cg + jnp.minimum(ce, need), (SUBLANES, LANES)).astype(I32)
        return cg + jnp.sum(gt, axis=0, keepdims=True), ce + jnp.sum(eq, axis=0, keepdims=True)

    zero = jnp.zeros((1, LANES), F32)
    lax.fori_loop(0, s // LANES, chunk_body, (zero, zero))


def _topk_positions(probs3, cap):
    b, s, _ = probs3.shape
    kern = functools.partial(_topk_kernel, s=s, cap=cap)
    tok = pl.BlockSpec((None, s, LANES), lambda bb: (bb, 0, 0))
    return pl.pallas_call(
        kern,
        grid=(b,),
        in_specs=[tok],
        out_specs=[tok, tok, pl.BlockSpec((None, s // LANES, SUBLANES, LANES), lambda bb: (bb, 0, 0, 0))],
        out_shape=[jax.ShapeDtypeStruct((b, s, LANES), I32), jax.ShapeDtypeStruct((b, s, LANES), F32),
                   jax.ShapeDtypeStruct((b, s // LANES, SUBLANES, LANES), I32)],
        compiler_params=_params(1),
        name="expert_choice_topk",
    )(probs3)


def _slots_kernel(pst_ref, incl_ref, idx_ref, bc_scr, *, s, cap, n_exp, slab):
    rb = _tile(s, 512)
    n_chunk = s // LANES
    bb = pl.program_id(0)
    idx_ref[...] = jnp.zeros_like(idx_ref)
    first_row = bb * s
    lane_f = lax.broadcasted_iota(I32, (1, LANES), 1).astype(F32)
    for e in range(n_exp):
        def fill(i, carry):
            r0 = pl.multiple_of(i * rb, rb)
            bc_scr[pl.ds(r0, rb), :] = jnp.broadcast_to(incl_ref[pl.ds(r0, rb), e:e + 1], (rb, LANES))
            return carry

        lax.fori_loop(0, s // rb, fill, 0)

        def chosen_before(chunk):
            return pst_ref[(bb * n_chunk + jnp.minimum(chunk, n_chunk - 1)) * n_exp + e]

        def slot_tile(jt, carry):
            full, reach = carry
            lo = jt * LANES
            full = lax.while_loop(lambda t: (t < n_chunk - 1) & (chosen_before(t + 1) <= lo), lambda t: t + 1, full)
            reach = lax.while_loop(lambda t: (t < n_chunk) & (chosen_before(t) <= lo + LANES - 1),
                                   lambda t: t + 1, reach)
            slot = lane_f + lax.convert_element_type(lo, F32)

            def body(t, acc):
                blk = bc_scr[pl.ds(pl.multiple_of(t * LANES, LANES), LANES), :]
                return acc + jnp.sum(jnp.where(blk <= slot, 1.0, 0.0).reshape(-1, COUNT_ROWS, LANES), axis=0)

            acc = lax.fori_loop(full, reach, body, jnp.zeros((COUNT_ROWS, LANES), F32))
            token = jnp.sum(acc, axis=0, keepdims=True).astype(I32) + full * LANES
            idx_ref[e:e + 1, pl.ds(pl.multiple_of(lo, LANES), LANES)] = (token + first_row) * slab
            return full, reach

        lax.fori_loop(0, cap // LANES, slot_tile, (jnp.int32(0), jnp.int32(0)))


def _slot_tokens(pst_flat, incl, cap, n_exp, slab):
    b, s, _ = incl.shape
    width = _index_row_width(cap)
    kern = functools.partial(_slots_kernel, s=s, cap=cap, n_exp=n_exp, slab=slab)
    grid_spec = pltpu.PrefetchScalarGridSpec(
        num_scalar_prefetch=1,
        grid=(b,),
        in_specs=[pl.BlockSpec((None, s, LANES), lambda bb, pst: (bb, 0, 0))],
        out_specs=pl.BlockSpec((None, n_exp, width), lambda bb, pst: (bb, 0, 0)),
        scratch_shapes=[pltpu.VMEM((s, LANES), F32)],
    )
    return pl.pallas_call(
        kern,
        grid_spec=grid_spec,
        out_shape=jax.ShapeDtypeStruct((b, n_exp, width), I32),
        compiler_params=_params(1),
        name="expert_choice_slots",
    )(pst_flat, incl)


def _row_copy(u_hbm, xs_scr, sem, src_slab_row, dst_row, slab):
    src = u_hbm.at[pl.ds(pl.multiple_of(src_slab_row, SUBLANES), slab)]
    return pltpu.make_async_copy(src, xs_scr.at[pl.ds(dst_row * slab, slab)], sem)


def _expert_kernel(idx_ref, idx_next_ref, u_hbm, wg_ref, wu_ref, wd_ref, y_ref, xs_scr, xb_scr, gate_scr, acc_scr,
                   sem, *, ts, slab):
    e = pl.program_id(1)
    j = pl.program_id(2)
    fk = pl.program_id(3)
    n_j = pl.num_programs(2)
    tile = (pl.program_id(0) * pl.num_programs(1) + e) * n_j + j
    n_tiles = pl.num_programs(0) * pl.num_programs(1) * n_j
    data_rows = slab - SUBLANES

    def issue(ref, base):
        def body(r, carry):
            _row_copy(u_hbm, xs_scr, sem, ref[base + r], r, slab).start()
            return carry

        lax.fori_loop(0, ts, body, 0, unroll=8)

    @pl.when(fk == 0)
    def _():
        @pl.when(tile == 0)
        def _():
            issue(idx_ref, 0)

        def drain(r, carry):
            _row_copy(u_hbm, xs_scr, sem, 0, r, slab).wait()
            return carry

        lax.fori_loop(0, ts, drain, 0, unroll=8)
        for cc in range(data_rows):
            xb_scr[:, cc * LANES:(cc + 1) * LANES] = xs_scr[pl.ds(cc, ts, stride=slab), :].astype(BF16)
        probs = xs_scr[pl.ds(data_rows, ts, stride=slab), :]
        lane = lax.broadcasted_iota(I32, probs.shape, 1)
        gate_scr[...] = jnp.sum(jnp.where(lane == e, probs, 0.0), axis=1, keepdims=True)
        acc_scr[...] = jnp.zeros_like(acc_scr)

        @pl.when(j + 1 < n_j)
        def _():
            issue(idx_ref, (j + 1) * ts)

        @pl.when((j + 1 == n_j) & (tile + 1 < n_tiles))
        def _():
            issue(idx_next_ref, 0)

    xs = xb_scr[...]
    gate = _dot(xs, wg_ref[...])
    hid = (gate * jax.nn.sigmoid(gate) * _dot(xs, wu_ref[...])).astype(BF16)
    acc_scr[...] += _dot(hid, wd_ref[...])

    @pl.when(fk == pl.num_programs(3) - 1)
    def _():
        y_ref[...] = (acc_scr[...] * gate_scr[...]).astype(y_ref.dtype)


def _experts(idx_flat, u_slabs, wg, wu, wd, b, cap):
    n_exp, d, f = wg.shape
    slab = _slab_rows(d)
    ts, tf = _tile(cap, 1024), _tile(f, 512)
    width = _index_row_width(cap)
    last = b * n_exp - 1
    kern = functools.partial(_expert_kernel, ts=ts, slab=slab)
    return pl.pallas_call(
        kern,
        grid=(b, n_exp, cap // ts, f // tf),
        in_specs=[pl.BlockSpec((width,), lambda bb, e, j, k: (bb * n_exp + e,), memory_space=pltpu.SMEM),
                  pl.BlockSpec((width,), lambda bb, e, j, k: (jnp.minimum(bb * n_exp + e + 1, last),),
                               memory_space=pltpu.SMEM),
                  pl.BlockSpec(memory_space=pl.ANY),
                  pl.BlockSpec((None, d, tf), lambda bb, e, j, k: (e, 0, k)),
                  pl.BlockSpec((None, d, tf), lambda bb, e, j, k: (e, 0, k)),
                  pl.BlockSpec((None, tf, d), lambda bb, e, j, k: (e, k, 0))],
        out_specs=pl.BlockSpec((None, None, ts, d), lambda bb, e, j, k: (bb, e, j, 0)),
        out_shape=jax.ShapeDtypeStruct((b, n_exp, cap, d), BF16),
        scratch_shapes=[pltpu.VMEM((ts * slab, LANES), F32), pltpu.VMEM((ts, d), BF16), pltpu.VMEM((ts, 1), F32),
                        pltpu.VMEM((ts, d), F32), pltpu.SemaphoreType.DMA(())],
        compiler_params=_params(4),
        name="expert_ffn",
    )(idx_flat, idx_flat, u_slabs, wg, wu, wd)


def _combine_kernel(pst_ref, h_ref, sp_ref, y_hbm, nw_ref, o_ref, buf, sem, *, n_exp, cap, window, final):
    n_tiles = pl.num_programs(1)
    step = pl.program_id(0) * n_tiles + pl.program_id(1)
    n_steps = pl.num_programs(0) * n_tiles

    def window_start(at_step, e):
        first = pst_ref[at_step * n_exp + e] & -BF16_ROWS
        return pl.multiple_of(jnp.minimum(first, cap - window), BF16_ROWS)

    def window_copy(at_step, e):
        half = at_step % 2
        src = y_hbm.at[at_step // n_tiles, e, pl.ds(window_start(at_step, e), window)]
        return pltpu.make_async_copy(src, buf.at[half, pl.ds(e * window, window)], sem.at[half, e])

    @pl.when(step == 0)
    def _():
        for e in range(n_exp):
            window_copy(step, e).start()

    @pl.when(step + 1 < n_steps)
    def _():
        for e in range(n_exp):
            window_copy(step + 1, e).start()

    tm = h_ref.shape[0]
    lane = lax.broadcasted_iota(I32, (tm, LANES), 1)
    hits = [None] * (n_exp * window // LANES)
    for e in range(n_exp):
        window_copy(step, e).wait()
        slot = sp_ref[:, e:e + 1]
        target = jnp.where(slot >= 0, slot - window_start(step, e) + e * window, -1)
        for col in range(e * window // LANES, (e * window + window - 1) // LANES + 1):
            hit = target == lane + col * LANES
            hits[col] = hit if hits[col] is None else hits[col] | hit
    onehot = jnp.concatenate([jnp.where(hit, 1.0, 0.0).astype(BF16) for hit in hits], axis=1)
    acc = h_ref[...] + _dot(onehot, buf[step % 2])
    if final:
        acc = _rms(acc, nw_ref[...])
    o_ref[...] = acc


def _combine(pst_flat, h, sp, y, nw, n_exp, cap, final):
    b, s, d = h.shape
    tm = LANES
    window = tm + BF16_ROWS
    assert (n_exp * window) % LANES == 0
    kern = functools.partial(_combine_kernel, n_exp=n_exp, cap=cap, window=window, final=final)
    tok = lambda w: pl.BlockSpec((None, tm, w), lambda bb, i, pst: (bb, i, 0))
    grid_spec = pltpu.PrefetchScalarGridSpec(
        num_scalar_prefetch=1,
        grid=(b, s // tm),
        in_specs=[tok(d), tok(LANES), pl.BlockSpec(memory_space=pl.ANY),
                  pl.BlockSpec((1, d), lambda bb, i, pst: (0, 0))],
        out_specs=tok(d),
        scratch_shapes=[pltpu.VMEM((2, n_exp * window, d), BF16), pltpu.SemaphoreType.DMA((2, n_exp))],
    )
    return pl.pallas_call(
        kern,
        grid_spec=grid_spec,
        out_shape=jax.ShapeDtypeStruct((b, s, d), F32),
        compiler_params=_params(2),
        name="moe_combine",
    )(pst_flat, h, sp, y, nw)


def _lane_row(vals):
    return jnp.zeros((1, LANES), F32).at[0, :vals.shape[0]].set(vals.astype(F32))


def kernel(x, norm_mix, norm_ffn, norm_final, w_in, conv_w, gdn_a_log, gdn_dt_bias, gdn_norm, hgrn_lower_bounds,
           hgrn_norm, w_out, w_router, w_gate, w_up, w_down):
    b, s, d = x.shape
    depth = w_in.shape[0]
    ha = gdn_a_log.shape[-1]
    hd = gdn_norm.shape[-1]
    gw = ha * hd
    hw = hgrn_lower_bounds.shape[-1]
    hb = hw // hd
    n_exp = w_router.shape[-1]
    cap = CAPACITY_FACTOR * s // n_exp
    assert hd == LANES and gw == hw and s % LANES == 0 and 4 * ha <= LANES and n_exp <= LANES
    assert cap >= LANES + BF16_ROWS and cap % LANES == 0
    t = b * s
    sizes = (gw, gw, gw, gw, 2 * ha, 2 * ha, hw, 2 * hw, hw, hw)
    offs = [0]
    for sz in sizes:
        offs.append(offs[-1] + sz)
    lb_dir_major = jnp.transpose(hgrn_lower_bounds, (1, 0, 2))
    h = x.reshape(t, d)
    out = None
    for l in range(depth):
        wi = w_in[l]
        part = lambda n: wi[:, offs[n]:offs[n + 1]]
        w_main = jnp.concatenate([part(0), part(1), part(2), part(3), part(6), part(7), part(8), part(9)],
                                 axis=1).astype(BF16)
        w_small = jnp.zeros((d, LANES), F32).at[:, :4 * ha].set(
            jnp.concatenate([part(4), part(5)], axis=1)).astype(BF16)
        proj, gs = _in_projection(h, norm_mix[l][None, :], w_main, w_small)
        proj3 = proj.reshape(b, s, -1)
        gs3 = gs.reshape(b, s, LANES)
        qkv = _gdn_conv(proj3, conv_w[l], gw, hd)
        alog_row = jnp.zeros((1, LANES), F32).at[0, 2 * ha:4 * ha].set(gdn_a_log[l].reshape(-1))
        dt_row = jnp.zeros((1, LANES), F32).at[0, 2 * ha:4 * ha].set(gdn_dt_bias[l].reshape(-1))
        oa = [_gdn_scan(qkv, gs3, alog_row, dt_row, ha, hd, dd).reshape(t, gw) for dd in (0, 1)]
        ob = [_hgrn_scan(proj3, lb_dir_major, hb, hd, dd, l, 4, 5 + dd, 7).reshape(t, hw) for dd in (0, 1)]
        wr_f32 = jnp.zeros((d, LANES), F32).at[:, :n_exp].set(w_router[l])
        wr_hi = wr_f32.astype(BF16)
        wr_pad = jnp.concatenate([wr_hi, (wr_f32 - wr_hi.astype(F32)).astype(BF16)], axis=1)
        hn, u, probs = _out_projection(oa[0], oa[1], ob[0], ob[1], proj, h, gdn_norm[l][None, :],
                                       hgrn_norm[l][None, :], w_out[l].astype(BF16), norm_ffn[l][None, :],
                                       wr_pad, ha, hb, hd, 3, 8, n_exp)
        sp, incl, pst = _topk_positions(probs.reshape(b, s, LANES), cap)
        pst_flat = pst[:, :, 0, :n_exp].reshape(-1)
        idx = _slot_tokens(pst_flat, incl, cap, n_exp, _slab_rows(d))
        y = _experts(idx.reshape(-1), u, w_gate[l].astype(BF16), w_up[l].astype(BF16), w_down[l].astype(BF16),
                     b, cap)
        out = _combine(pst_flat, hn.reshape(b, s, d), sp, y, norm_final[None, :], n_exp, cap, l == depth - 1)
        h = out.reshape(t, d)
    return out
```

```python
import functools

import jax
import jax.numpy as jnp
from jax import lax
from jax.experimental import pallas as pl
from jax.experimental.pallas import tpu as pltpu

F32, BF16, I32 = jnp.float32, jnp.bfloat16, jnp.int32
NORM_EPS = 1e-6
CHUNK = 64
CAPACITY_FACTOR = 2
LANES = 128
SUBLANES = 8
BF16_ROWS = 16
SMEM_BLOCK = 1024
COUNT_ROWS = 8 * SUBLANES
VMEM_LIMIT = 56 * 1024 * 1024
NT_DIMS = (((1,), (1,)), ((), ()))
TN_DIMS = (((0,), (0,)), ((), ()))


def _tile(n, pref):
    return pref if n % pref == 0 else n


def _params(n_axes):
    return pltpu.CompilerParams(dimension_semantics=("arbitrary",) * n_axes,
                                vmem_limit_bytes=VMEM_LIMIT)


def _dot(a, b):
    return jnp.dot(a, b, preferred_element_type=F32)


def _dot_nt(a, b):
    return lax.dot_general(a, b, NT_DIMS, preferred_element_type=F32)


def _dot_tn(a, b):
    return lax.dot_general(a, b, TN_DIMS, preferred_element_type=F32)


def _rms(x, w):
    return x * lax.rsqrt(jnp.mean(x * x, axis=-1, keepdims=True) + NORM_EPS) * w


def _softplus(x):
    return jnp.maximum(x, 0.0) + jnp.log(1.0 + jnp.exp(-jnp.abs(x)))


def _cumsum_rows(x, rev):
    n = x.shape[0]
    row = lax.broadcasted_iota(I32, x.shape, 0)
    k = 1
    while k < n:
        if rev:
            x = x + jnp.where(row < n - k, pltpu.roll(x, n - k, 0), 0.0)
        else:
            x = x + jnp.where(row >= k, pltpu.roll(x, k, 0), 0.0)
        k *= 2
    return x


def _inproj_kernel(x_ref, nw_ref, w_ref, ws_ref, o_ref, os_ref, u_scr):
    @pl.when(pl.program_id(1) == 0)
    def _():
        ub = _rms(x_ref[...], nw_ref[...]).astype(BF16)
        u_scr[...] = ub
        os_ref[...] = _dot(ub, ws_ref[...])

    o_ref[...] = _dot(u_scr[...], w_ref[...])


def _in_projection(h, nw, w_main, w_small):
    t, d = h.shape
    n = w_main.shape[1]
    tm, tn = _tile(t, 1024), _tile(n, 1024)
    return pl.pallas_call(
        _inproj_kernel,
        grid=(t // tm, n // tn),
        in_specs=[pl.BlockSpec((tm, d), lambda i, j: (i, 0)),
                  pl.BlockSpec((1, d), lambda i, j: (0, 0)),
                  pl.BlockSpec((d, tn), lambda i, j: (0, j)),
                  pl.BlockSpec((d, LANES), lambda i, j: (0, 0))],
        out_specs=[pl.BlockSpec((tm, tn), lambda i, j: (i, j)),
                   pl.BlockSpec((tm, LANES), lambda i, j: (i, 0))],
        out_shape=[jax.ShapeDtypeStruct((t, n), F32), jax.ShapeDtypeStruct((t, LANES), F32)],
        scratch_shapes=[pltpu.VMEM((tm, d), BF16)],
        compiler_params=_params(2),
        name="in_projection",
    )(h, nw, w_main, w_small)


def _conv_kernel(prev_ref, x_ref, next_ref, w_ref, o_ref, ext_scr, *, rows, nblk, taps, hd):
    i = pl.program_id(1)
    g = pl.program_id(2)
    pad = SUBLANES
    ext_scr[0:pad, :] = jnp.where(i > 0, prev_ref[...], 0.0)
    ext_scr[pad:pad + rows, :] = x_ref[...]
    ext_scr[pad + rows:, :] = jnp.where(i < nblk - 1, next_ref[...], 0.0)
    first = pad - (taps - 1) // 2
    acc = ext_scr[pl.ds(first, rows), :] * w_ref[0:1, :]
    for j in range(1, taps):
        acc = acc + ext_scr[pl.ds(first + j, rows), :] * w_ref[j:j + 1, :]
    y = acc * jax.nn.sigmoid(acc)
    qscale = jnp.where(g == 0, hd ** -0.5, 1.0)
    for hh in range(y.shape[1] // hd):
        sl = slice(hh * hd, (hh + 1) * hd)
        yh = y[:, sl]
        r = lax.rsqrt(jnp.sum(yh * yh, axis=-1, keepdims=True) + NORM_EPS) * qscale
        o_ref[:, sl] = jnp.where(g == 2, yh, yh * r)


def _gdn_conv(proj3, conv_w, gw, hd):
    b, s, _ = proj3.shape
    taps = conv_w.shape[0]
    rows = _tile(s, 512)
    nblk = s // rows
    rb = rows // SUBLANES
    wpad = jnp.zeros((SUBLANES, 3 * gw), F32).at[:taps].set(conv_w)
    kern = functools.partial(_conv_kernel, rows=rows, nblk=nblk, taps=taps, hd=hd)
    return pl.pallas_call(
        kern,
        grid=(b, nblk, 3),
        in_specs=[pl.BlockSpec((None, SUBLANES, gw), lambda bb, i, g: (bb, jnp.maximum(i * rb - 1, 0), g)),
                  pl.BlockSpec((None, rows, gw), lambda bb, i, g: (bb, i, g)),
                  pl.BlockSpec((None, SUBLANES, gw),
                               lambda bb, i, g: (bb, jnp.minimum((i + 1) * rb, s // SUBLANES - 1), g)),
                  pl.BlockSpec((SUBLANES, gw), lambda bb, i, g: (0, g))],
        out_specs=pl.BlockSpec((None, rows, gw), lambda bb, i, g: (bb, i, g)),
        out_shape=jax.ShapeDtypeStruct((b, s, 3 * gw), F32),
        scratch_shapes=[pltpu.VMEM((rows + 2 * SUBLANES, gw), F32)],
        compiler_params=_params(3),
        name="gdn_conv",
    )(proj3, proj3, proj3, wpad)


def _head_group(heads):
    return 2 if heads % 2 == 0 else 1


def _pick(parts, block_id):
    out = parts[0]
    for i in range(1, len(parts)):
        out = jnp.where(block_id == i, parts[i], out)
    return out


def _gdn_kernel(q_ref, k_ref, v_ref, gs_ref, alog_ref, dt_ref, o_ref, s_scr, *, heads, hd, rev, d, nsub):
    c = CHUNK
    grp = _head_group(heads)
    r = grp * c
    groups = [list(range(g * grp, (g + 1) * grp)) for g in range(heads // grp)]
    units = [(sc, g) for sc in range(nsub) for g in range(len(groups))]
    shift = c.bit_length() - 1

    @pl.when(pl.program_id(1) == 0)
    def _():
        s_scr[...] = jnp.zeros_like(s_scr)

    gs = gs_ref[...]
    gall = -jnp.exp(alog_ref[...]) * _softplus(gs + dt_ref[...])
    bt_all = jax.nn.sigmoid(gs)
    row_blk = lax.broadcasted_iota(I32, (r, 1), 0) >> shift
    lane_blk = lax.broadcasted_iota(I32, (1, r), 1) >> shift
    ti = lax.broadcasted_iota(I32, (r, r), 0)
    si = lax.broadcasted_iota(I32, (r, r), 1)
    same = (ti >> shift) == (si >> shift)
    tl = ti & (c - 1)
    sl = si & (c - 1)
    incl = same & ((tl <= sl) if rev else (tl >= sl))
    strict = same & ((tl < sl) if rev else (tl > sl))
    eye = (ti == si).astype(F32)
    last = 0 if rev else c - 1
    gcol = lambda h: 2 * heads + d * heads + h
    gc1s, gcs, gcts, bts = [], [], [], []
    for sc in range(nsub):
        gc1 = _cumsum_rows(gall[sc * c:(sc + 1) * c], rev)
        gc = jnp.concatenate([gc1] * grp, axis=0)
        gc1s.append(gc1)
        gcs.append(gc)
        gcts.append(gc.T)
        bts.append(jnp.concatenate([bt_all[sc * c:(sc + 1) * c]] * grp, axis=0))

    def stack(ref, sc, hs):
        return jnp.concatenate([ref[sc * c:(sc + 1) * c, h * hd:(h + 1) * hd] for h in hs], axis=0)

    qs, ks, vs, gls, ys, ps, qks = [], [], [], [], [], [], []
    for sc, g in units:
        hs = groups[g]
        gc, gct, bt = gcs[sc], gcts[sc], bts[sc]
        gcc = _pick([gc[:, gcol(h):gcol(h) + 1] for h in hs], row_blk)
        gcr = _pick([gct[gcol(h):gcol(h) + 1, :] for h in hs], lane_blk)
        beta = _pick([bt[:, d * heads + h:d * heads + h + 1] for h in hs], row_blk)
        gl = _pick([gc1s[sc][last:last + 1, gcol(h):gcol(h) + 1] for h in hs], row_blk)
        q, k, v = stack(q_ref, sc, hs), stack(k_ref, sc, hs), stack(v_ref, sc, hs)
        decay = jnp.where(incl, jnp.exp(jnp.minimum(gcc - gcr, 0.0)), 0.0)
        eg = jnp.exp(gcc)
        kb = k * beta
        kbf = k.astype(BF16)
        y = -jnp.where(strict, _dot_nt(kb.astype(BF16), kbf) * decay, 0.0)
        qks.append((_dot_nt(q.astype(BF16), kbf) * decay).astype(BF16))
        qs.append(q * eg)
        ks.append(k * jnp.exp(gl - gcc))
        vs.append(jnp.concatenate([v * beta, kb * eg], axis=1).astype(BF16))
        gls.append(gl)
        ys.append(y)
        ps.append(eye + y)

    span = 2
    while span < c:
        ybs = [y.astype(BF16) for y in ys]
        ys = [_dot(yb, yb) for yb in ybs]
        ps = [p + _dot(p.astype(BF16), y.astype(BF16)) for p, y in zip(ps, ys)]
        span *= 2

    uws = [_dot(p.astype(BF16), rhs) for p, rhs in zip(ps, vs)]

    states = [s_scr[h] for h in range(heads)]
    for sc in (range(nsub - 1, -1, -1) if rev else range(nsub)):
        wss = {}
        for g, hs in enumerate(groups):
            un = sc * len(groups) + g
            for i, h in enumerate(hs):
                rows = slice(i * c, (i + 1) * c)
                lhs = jnp.concatenate([uws[un][rows, hd:], qs[un][rows, :]], axis=0).astype(BF16)
                wss[h] = _dot(lhs, states[h].astype(BF16))
        for g, hs in enumerate(groups):
            un = sc * len(groups) + g
            vnb = jnp.concatenate([uws[un][i * c:(i + 1) * c, :hd] - wss[h][:c] for i, h in enumerate(hs)],
                                  axis=0).astype(BF16)
            intra = _dot(qks[un], vnb)
            for i, h in enumerate(hs):
                rows = slice(i * c, (i + 1) * c)
                o_ref[sc * c:(sc + 1) * c, h * hd:(h + 1) * hd] = wss[h][c:] + intra[rows]
                states[h] = jnp.exp(gls[un][i * c:i * c + 1]) * states[h] + _dot_tn(
                    ks[un][rows].astype(BF16), vnb[rows])
    for h in range(heads):
        s_scr[h] = states[h]


def _scan_chunks_per_step(n_chunks, pref):
    return pref if n_chunks % pref == 0 else 1


def _gdn_scan(qkv, gs3, alog_row, dt_row, heads, hd, d):
    b, s, _ = qkv.shape
    gw = heads * hd
    nsub = _scan_chunks_per_step(s // CHUNK, 4)
    rows = nsub * CHUNK
    nblk = s // rows
    rev = d == 1
    cidx = (lambda cc: nblk - 1 - cc) if rev else (lambda cc: cc)
    kern = functools.partial(_gdn_kernel, heads=heads, hd=hd, rev=rev, d=d, nsub=nsub)
    col = lambda g: pl.BlockSpec((None, rows, gw), lambda bb, cc: (bb, cidx(cc), g))
    row = pl.BlockSpec((1, LANES), lambda bb, cc: (0, 0))
    return pl.pallas_call(
        kern,
        grid=(b, nblk),
        in_specs=[col(0), col(1), col(2),
                  pl.BlockSpec((None, rows, LANES), lambda bb, cc: (bb, cidx(cc), 0)), row, row],
        out_specs=pl.BlockSpec((None, rows, gw), lambda bb, cc: (bb, cidx(cc), 0)),
        out_shape=jax.ShapeDtypeStruct((b, s, gw), F32),
        scratch_shapes=[pltpu.VMEM((heads, hd, hd), F32)],
        compiler_params=_params(2),
        name="gdn_scan_bwd" if rev else "gdn_scan_fwd",
    )(qkv, qkv, qkv, gs3, alog_row, dt_row)


def _hgrn_kernel(q_ref, f_ref, v_ref, lbraw_ref, o_ref, s_scr, *, heads, hd, rev, layer, nsub):
    c = CHUNK
    grp = _head_group(heads)
    r = grp * c
    groups = [list(range(g * grp, (g + 1) * grp)) for g in range(heads // grp)]
    units = [(sc, g) for sc in range(nsub) for g in range(len(groups))]

    @pl.when(pl.program_id(1) == 0)
    def _():
        s_scr[...] = jnp.zeros_like(s_scr)

    ti = lax.broadcasted_iota(I32, (r, r), 0)
    si = lax.broadcasted_iota(I32, (r, r), 1)
    row = lax.broadcasted_iota(I32, (r, hd), 0)
    raw = lbraw_ref[...]
    ex = jnp.exp(raw - jnp.max(raw, axis=0, keepdims=True))
    share = ex / jnp.sum(ex, axis=0, keepdims=True)
    lb_all = jnp.zeros((1, share.shape[1]), F32)
    for j in range(1, layer + 1):
        lb_all = lb_all + share[j:j + 1, :]
    tiny = jnp.finfo(F32).tiny

    def stack(ref, sc, hs):
        return jnp.concatenate([ref[sc * c:(sc + 1) * c, h * hd:(h + 1) * hd] for h in hs], axis=0)

    qs, ks, vbs, ips, ess, tots, attns = [], [], [], [], [], [], []
    for sc, g in units:
        hs = groups[g]
        q, fpre = stack(q_ref, sc, hs), stack(f_ref, sc, hs)
        lb = jnp.concatenate([jnp.broadcast_to(lb_all[:, h * hd:(h + 1) * hd], (c, hd)) for h in hs], axis=0)
        f = jnp.maximum(lb + (1.0 - lb) * jax.nn.sigmoid(fpre), tiny)
        k = (1.0 - lb) * jax.nn.sigmoid(-fpre)
        qs.append(q)
        ks.append(k)
        vbs.append(stack(v_ref, sc, hs).astype(BF16))
        ips.append(f)
        ess.append(jnp.ones_like(f))
        tots.append(f)
        attns.append(jnp.where(ti == si, _dot_nt(q.astype(BF16), k.astype(BF16)), 0.0))
    m, shift = 1, 0
    while m < c:
        tb = ti >> shift
        sb = si >> shift
        if rev:
            pair = ((tb & 1) == 0) & (sb == tb + 1)
        else:
            pair = ((tb & 1) == 1) & (sb == tb - 1)
        later = (lambda blk: blk % 2 == 0) if rev else (lambda blk: blk % 2 == 1)
        for un in range(len(units)):
            lvl = _dot_nt((qs[un] * ips[un]).astype(BF16), (ks[un] * ess[un]).astype(BF16))
            if m < SUBLANES:
                attns[un] = attns[un] + jnp.where(pair, lvl, 0.0)
            else:
                attns[un] = jnp.concatenate(
                    [attns[un][b0:b0 + m] + jnp.where(pair[b0:b0 + m], lvl[b0:b0 + m], 0.0)
                     if later(b0 // m) else attns[un][b0:b0 + m] for b0 in range(0, r, m)], axis=0)
        for un in range(len(units)):
            ip, es, tot = ips[un], ess[un], tots[un]
            if m < SUBLANES:
                upper = ((row >> shift) & 1) == 1
                below = pltpu.roll(tot, m, 0)
                above = pltpu.roll(tot, r - m, 0)
                if rev:
                    ips[un] = ip * jnp.where(upper, 1.0, above)
                    ess[un] = es * jnp.where(upper, below, 1.0)
                else:
                    ips[un] = ip * jnp.where(upper, below, 1.0)
                    ess[un] = es * jnp.where(upper, 1.0, above)
                tots[un] = tot * jnp.where(upper, below, above)
            else:
                blocks = [(b0, (b0 // m ^ 1) * m) for b0 in range(0, r, m)]
                ips[un] = jnp.concatenate([ip[b0:b0 + m] * tot[s0:s0 + m] if later(b0 // m) else ip[b0:b0 + m]
                                           for b0, s0 in blocks], axis=0)
                ess[un] = jnp.concatenate([es[b0:b0 + m] if later(b0 // m) else es[b0:b0 + m] * tot[s0:s0 + m]
                                           for b0, s0 in blocks], axis=0)
                tots[un] = jnp.concatenate([tot[b0:b0 + m] * tot[s0:s0 + m] for b0, s0 in blocks], axis=0)
        m, shift = 2 * m, shift + 1
    intras = [_dot(attns[un].astype(BF16), vbs[un]) for un in range(len(units))]
    qis = [(qs[un] * ips[un]).astype(BF16) for un in range(len(units))]
    kes = [(ks[un] * ess[un]).astype(BF16) for un in range(len(units))]
    states = [s_scr[h] for h in range(heads)]
    for sc in (range(nsub - 1, -1, -1) if rev else range(nsub)):
        for g, hs in enumerate(groups):
            un = sc * len(groups) + g
            for i, h in enumerate(hs):
                rows = slice(i * c, (i + 1) * c)
                o_ref[sc * c:(sc + 1) * c, h * hd:(h + 1) * hd] = _dot_nt(
                    qis[un][rows], states[h].astype(BF16)) + intras[un][rows]
                states[h] = tots[un][i * c:i * c + 1, :] * states[h] + _dot_tn(vbs[un][rows], kes[un][rows])
    for h in range(heads):
        s_scr[h] = states[h]


def _hgrn_scan(proj3, lbraw, heads, hd, d, layer, qcol, fcol, vcol):
    b, s, _ = proj3.shape
    hw = heads * hd
    nsub = _scan_chunks_per_step(s // CHUNK, 4)
    rows = nsub * CHUNK
    nblk = s // rows
    rev = d == 1
    depth = lbraw.shape[1]
    cidx = (lambda cc: nblk - 1 - cc) if rev else (lambda cc: cc)
    kern = functools.partial(_hgrn_kernel, heads=heads, hd=hd, rev=rev, layer=layer, nsub=nsub)
    col = lambda g: pl.BlockSpec((None, rows, hw), lambda bb, cc: (bb, cidx(cc), g))
    return pl.pallas_call(
        kern,
        grid=(b, nblk),
        in_specs=[col(qcol), col(fcol), col(vcol),
                  pl.BlockSpec((None, depth, hw), lambda bb, cc: (d, 0, 0))],
        out_specs=pl.BlockSpec((None, rows, hw), lambda bb, cc: (bb, cidx(cc), 0)),
        out_shape=jax.ShapeDtypeStruct((b, s, hw), F32),
        scratch_shapes=[pltpu.VMEM((heads, hd, hd), F32)],
        compiler_params=_params(2),
        name="hgrn_scan_bwd" if rev else "hgrn_scan_fwd",
    )(proj3, proj3, proj3, lbraw)


def _outproj_kernel(oaf_ref, oab_ref, z_ref, obf_ref, obb_ref, g_ref, h_ref, gnw_ref, hnw_ref, wout_ref,
                    nfw_ref, wr_ref, hn_ref, u_ref, p_ref, mix_scr, *, ha, hb, hd, n_exp):
    def gated(of_ref, ob_ref, gate_ref, w_ref, heads, base):
        for hh in range(heads):
            sl = slice(hh * hd, (hh + 1) * hd)
            o = of_ref[:, sl] + ob_ref[:, sl]
            zz = gate_ref[:, sl]
            y = _rms(o, w_ref[...]) * (zz * jax.nn.sigmoid(zz))
            mix_scr[:, base + hh * hd:base + (hh + 1) * hd] = y.astype(BF16)

    gated(oaf_ref, oab_ref, z_ref, gnw_ref, ha, 0)
    gated(obf_ref, obb_ref, g_ref, hnw_ref, hb, ha * hd)
    hn = h_ref[...] + _dot(mix_scr[...], wout_ref[...])
    hn_ref[...] = hn
    u = _rms(hn, nfw_ref[...])
    u_hi = u.astype(BF16)
    u_lo = (u - u_hi.astype(F32)).astype(BF16)
    hi_terms = _dot(u_hi, wr_ref[...])
    logits = hi_terms[:, :LANES] + hi_terms[:, LANES:] + _dot(u_lo, wr_ref[:, :LANES])
    lane = lax.broadcasted_iota(I32, logits.shape, 1)
    logits = jnp.where(lane < n_exp, logits, -1e30)
    ex = jnp.exp(logits - jnp.max(logits, axis=-1, keepdims=True))
    ex = jnp.where(lane < n_exp, ex, 0.0)
    probs = ex / jnp.sum(ex, axis=-1, keepdims=True)
    p_ref[...] = probs
    data_rows = u.shape[1] // LANES
    slab = data_rows + SUBLANES
    for cc in range(data_rows):
        u_ref[pl.ds(cc, u.shape[0], stride=slab), :] = u[:, cc * LANES:(cc + 1) * LANES]
    for cc in range(data_rows, slab):
        u_ref[pl.ds(cc, u.shape[0], stride=slab), :] = probs


def _slab_rows(d):
    return d // LANES + SUBLANES


def _out_projection(oaf, oab, obf, obb, proj, h, gnw, hnw, wout, nfw, wr_pad, ha, hb, hd, zcol, gcol, n_exp):
    t, d = h.shape
    gw, hw = ha * hd, hb * hd
    tm = _tile(t, 256)
    kern = functools.partial(_outproj_kernel, ha=ha, hb=hb, hd=hd, n_exp=n_exp)
    rowblk = lambda w: pl.BlockSpec((tm, w), lambda i: (i, 0))
    full = lambda shp: pl.BlockSpec(shp, lambda i: (0, 0))
    return pl.pallas_call(
        kern,
        grid=(t // tm,),
        in_specs=[rowblk(gw), rowblk(gw), pl.BlockSpec((tm, gw), lambda i: (i, zcol)),
                  rowblk(hw), rowblk(hw), pl.BlockSpec((tm, hw), lambda i: (i, gcol)),
                  rowblk(d), full((1, hd)), full((1, hd)), full((gw + hw, d)), full((1, d)),
                  full((d, 2 * LANES))],
        out_specs=[rowblk(d), pl.BlockSpec((tm * _slab_rows(d), LANES), lambda i: (i, 0)), rowblk(LANES)],
        out_shape=[jax.ShapeDtypeStruct((t, d), F32), jax.ShapeDtypeStruct((t * _slab_rows(d), LANES), F32),
                   jax.ShapeDtypeStruct((t, LANES), F32)],
        scratch_shapes=[pltpu.VMEM((tm, gw + hw), BF16)],
        compiler_params=_params(1),
        name="out_projection_router",
    )(oaf, oab, proj, obf, obb, proj, h, gnw, hnw, wout, nfw, wr_pad)


def _index_row_width(cap):
    return -(-cap // SMEM_BLOCK) * SMEM_BLOCK


def _topk_kernel(p_ref, sp_ref, incl_ref, pst_ref, *, s, cap):
    rb = _tile(s, 512)

    def count(pred):
        def body(i, acc):
            bits = pltpu.bitcast(p_ref[pl.ds(pl.multiple_of(i * rb, rb), rb), :], I32)
            return acc + jnp.sum(pred(bits).astype(I32).reshape(rb // SUBLANES, SUBLANES, LANES), axis=0)

        acc = lax.fori_loop(0, s // rb, body, jnp.zeros((SUBLANES, LANES), I32))
        return jnp.sum(acc, axis=0, keepdims=True)

    def bit_body(j, thr):
        cand = thr | jnp.left_shift(jnp.int32(1), 30 - j)
        return jnp.where(count(lambda bits: bits >= cand) >= cap, cand, thr)

    thr = lax.fori_loop(0, 31, bit_body, jnp.zeros((1, LANES), I32))
    need = (cap - count(lambda bits: bits > thr)).astype(F32)
    ri = lax.broadcasted_iota(I32, (LANES, LANES), 0)
    ci = lax.broadcasted_iota(I32, (LANES, LANES), 1)
    before = (ci < ri).astype(BF16)

    def chunk_body(cc, carry):
        cg, ce = carry
        r0 = pl.multiple_of(cc * LANES, LANES)
        bits = pltpu.bitcast(p_ref[pl.ds(r0, LANES), :], I32)
        gt = (bits > thr).astype(F32)
        eq = (bits == thr).astype(F32)
        pg = cg + _dot(before, gt.astype(BF16))
        pe = ce + _dot(before, eq.astype(BF16))
        sel = (gt > 0.0) | ((eq > 0.0) & (pe < need))
        pos = pg + jnp.minimum(pe, need)
        sp_ref[pl.ds(r0, LANES), :] = jnp.where(sel, pos, -1.0).astype(I32)
        incl_ref[pl.ds(r0, LANES), :] = pos + jnp.where(sel, 1.0, 0.0)
        pst_ref[cc] = jnp.broadcast_to(cg + jnp.minimum(ce, need), (SUBLANES, LANES)).astype(I32)
        return cg + jnp.sum(gt, axis=0, keepdims=True), ce + jnp.sum(eq, axis=0, keepdims=True)

    zero = jnp.zeros((1, LANES), F32)
    lax.fori_loop(0, s // LANES, chunk_body, (zero, zero))


def _topk_positions(probs3, cap):
    b, s, _ = probs3.shape
    kern = functools.partial(_topk_kernel, s=s, cap=cap)
    tok = pl.BlockSpec((None, s, LANES), lambda bb: (bb, 0, 0))
    return pl.pallas_call(
        kern,
        grid=(b,),
        in_specs=[tok],
        out_specs=[tok, tok, pl.BlockSpec((None, s // LANES, SUBLANES, LANES), lambda bb: (bb, 0, 0, 0))],
        out_shape=[jax.ShapeDtypeStruct((b, s, LANES), I32), jax.ShapeDtypeStruct((b, s, LANES), F32),
                   jax.ShapeDtypeStruct((b, s // LANES, SUBLANES, LANES), I32)],
        compiler_params=_params(1),
        name="expert_choice_topk",
    )(probs3)


def _slots_kernel(pst_ref, incl_ref, idx_ref, bc_scr, *, s, cap, n_exp, slab):
    rb = _tile(s, 512)
    n_chunk = s // LANES
    bb = pl.program_id(0)
    idx_ref[...] = jnp.zeros_like(idx_ref)
    first_row = bb * s
    lane_f = lax.broadcasted_iota(I32, (1, LANES), 1).astype(F32)
    for e in range(n_exp):
        def fill(i, carry):
            r0 = pl.multiple_of(i * rb, rb)
            bc_scr[pl.ds(r0, rb), :] = jnp.broadcast_to(incl_ref[pl.ds(r0, rb), e:e + 1], (rb, LANES))
            return carry

        lax.fori_loop(0, s // rb, fill, 0)

        def chosen_before(chunk):
            return pst_ref[(bb * n_chunk + jnp.minimum(chunk, n_chunk - 1)) * n_exp + e]

        def slot_tile(jt, carry):
            full, reach = carry
            lo = jt * LANES
            full = lax.while_loop(lambda t: (t < n_chunk - 1) & (chosen_before(t + 1) <= lo), lambda t: t + 1, full)
            reach = lax.while_loop(lambda t: (t < n_chunk) & (chosen_before(t) <= lo + LANES - 1),
                                   lambda t: t + 1, reach)
            slot = lane_f + lax.convert_element_type(lo, F32)

            def body(t, acc):
                blk = bc_scr[pl.ds(pl.multiple_of(t * LANES, LANES), LANES), :]
                return acc + jnp.sum(jnp.where(blk <= slot, 1.0, 0.0).reshape(-1, COUNT_ROWS, LANES), axis=0)

            acc = lax.fori_loop(full, reach, body, jnp.zeros((COUNT_ROWS, LANES), F32))
            token = jnp.sum(acc, axis=0, keepdims=True).astype(I32) + full * LANES
            idx_ref[e:e + 1, pl.ds(pl.multiple_of(lo, LANES), LANES)] = (token + first_row) * slab
            return full, reach

        lax.fori_loop(0, cap // LANES, slot_tile, (jnp.int32(0), jnp.int32(0)))


def _slot_tokens(pst_flat, incl, cap, n_exp, slab):
    b, s, _ = incl.shape
    width = _index_row_width(cap)
    kern = functools.partial(_slots_kernel, s=s, cap=cap, n_exp=n_exp, slab=slab)
    grid_spec = pltpu.PrefetchScalarGridSpec(
        num_scalar_prefetch=1,
        grid=(b,),
        in_specs=[pl.BlockSpec((None, s, LANES), lambda bb, pst: (bb, 0, 0))],
        out_specs=pl.BlockSpec((None, n_exp, width), lambda bb, pst: (bb, 0, 0)),
        scratch_shapes=[pltpu.VMEM((s, LANES), F32)],
    )
    return pl.pallas_call(
        kern,
        grid_spec=grid_spec,
        out_shape=jax.ShapeDtypeStruct((b, n_exp, width), I32),
        compiler_params=_params(1),
        name="expert_choice_slots",
    )(pst_flat, incl)


def _row_copy(u_hbm, xs_scr, sem, src_slab_row, dst_row, slab):
    src = u_hbm.at[pl.ds(pl.multiple_of(src_slab_row, SUBLANES), slab)]
    return pltpu.make_async_copy(src, xs_scr.at[pl.ds(dst_row * slab, slab)], sem)


def _expert_kernel(idx_ref, idx_next_ref, u_hbm, wg_ref, wu_ref, wd_ref, y_ref, xs_scr, xb_scr, gate_scr, acc_scr,
                   sem, *, ts, slab):
    e = pl.program_id(1)
    j = pl.program_id(2)
    fk = pl.program_id(3)
    n_j = pl.num_programs(2)
    tile = (pl.program_id(0) * pl.num_programs(1) + e) * n_j + j
    n_tiles = pl.num_programs(0) * pl.num_programs(1) * n_j
    data_rows = slab - SUBLANES

    def issue(ref, base):
        def body(r, carry):
            _row_copy(u_hbm, xs_scr, sem, ref[base + r], r, slab).start()
            return carry

        lax.fori_loop(0, ts, body, 0, unroll=8)

    @pl.when(fk == 0)
    def _():
        @pl.when(tile == 0)
        def _():
            issue(idx_ref, 0)

        def drain(r, carry):
            _row_copy(u_hbm, xs_scr, sem, 0, r, slab).wait()
            return carry

        lax.fori_loop(0, ts, drain, 0, unroll=8)
        for cc in range(data_rows):
            xb_scr[:, cc * LANES:(cc + 1) * LANES] = xs_scr[pl.ds(cc, ts, stride=slab), :].astype(BF16)
        probs = xs_scr[pl.ds(data_rows, ts, stride=slab), :]
        lane = lax.broadcasted_iota(I32, probs.shape, 1)
        gate_scr[...] = jnp.sum(jnp.where(lane == e, probs, 0.0), axis=1, keepdims=True)
        acc_scr[...] = jnp.zeros_like(acc_scr)

        @pl.when(j + 1 < n_j)
        def _():
            issue(idx_ref, (j + 1) * ts)

        @pl.when((j + 1 == n_j) & (tile + 1 < n_tiles))
        def _():
            issue(idx_next_ref, 0)

    xs = xb_scr[...]
    gate = _dot(xs, wg_ref[...])
    hid = (gate * jax.nn.sigmoid(gate) * _dot(xs, wu_ref[...])).astype(BF16)
    acc_scr[...] += _dot(hid, wd_ref[...])

    @pl.when(fk == pl.num_programs(3) - 1)
    def _():
        y_ref[...] = (acc_scr[...] * gate_scr[...]).astype(y_ref.dtype)


def _experts(idx_flat, u_slabs, wg, wu, wd, b, cap):
    n_exp, d, f = wg.shape
    slab = _slab_rows(d)
    ts, tf = _tile(cap, 1024), _tile(f, 512)
    width = _index_row_width(cap)
    last = b * n_exp - 1
    kern = functools.partial(_expert_kernel, ts=ts, slab=slab)
    return pl.pallas_call(
        kern,
        grid=(b, n_exp, cap // ts, f // tf),
        in_specs=[pl.BlockSpec((width,), lambda bb, e, j, k: (bb * n_exp + e,), memory_space=pltpu.SMEM),
                  pl.BlockSpec((width,), lambda bb, e, j, k: (jnp.minimum(bb * n_exp + e + 1, last),),
                               memory_space=pltpu.SMEM),
                  pl.BlockSpec(memory_space=pl.ANY),
                  pl.BlockSpec((None, d, tf), lambda bb, e, j, k: (e, 0, k)),
                  pl.BlockSpec((None, d, tf), lambda bb, e, j, k: (e, 0, k)),
                  pl.BlockSpec((None, tf, d), lambda bb, e, j, k: (e, k, 0))],
        out_specs=pl.BlockSpec((None, None, ts, d), lambda bb, e, j, k: (bb, e, j, 0)),
        out_shape=jax.ShapeDtypeStruct((b, n_exp, cap, d), BF16),
        scratch_shapes=[pltpu.VMEM((ts * slab, LANES), F32), pltpu.VMEM((ts, d), BF16), pltpu.VMEM((ts, 1), F32),
                        pltpu.VMEM((ts, d), F32), pltpu.SemaphoreType.DMA(())],
        compiler_params=_params(4),
        name="expert_ffn",
    )(idx_flat, idx_flat, u_slabs, wg, wu, wd)


def _combine_kernel(pst_ref, h_ref, sp_ref, y_hbm, nw_ref, o_ref, buf, sem, *, n_exp, cap, window, final):
    n_tiles = pl.num_programs(1)
    step = pl.program_id(0) * n_tiles + pl.program_id(1)
    n_steps = pl.num_programs(0) * n_tiles

    def window_start(at_step, e):
        first = pst_ref[at_step * n_exp + e] & -BF16_ROWS
        return pl.multiple_of(jnp.minimum(first, cap - window), BF16_ROWS)

    def window_copy(at_step, e):
        half = at_step % 2
        src = y_hbm.at[at_step // n_tiles, e, pl.ds(window_start(at_step, e), window)]
        return pltpu.make_async_copy(src, buf.at[half, pl.ds(e * window, window)], sem.at[half, e])

    @pl.when(step == 0)
    def _():
        for e in range(n_exp):
            window_copy(step, e).start()

    @pl.when(step + 1 < n_steps)
    def _():
        for e in range(n_exp):
            window_copy(step + 1, e).start()

    tm = h_ref.shape[0]
    lane = lax.broadcasted_iota(I32, (tm, LANES), 1)
    hits = [None] * (n_exp * window // LANES)
    for e in range(n_exp):
        window_copy(step, e).wait()
        slot = sp_ref[:, e:e + 1]
        target = jnp.where(slot >= 0, slot - window_start(step, e) + e * window, -1)
        for col in range(e * window // LANES, (e * window + window - 1) // LANES + 1):
            hit = target == lane + col * LANES
            hits[col] = hit if hits[col] is None else hits[col] | hit
    onehot = jnp.concatenate([jnp.where(hit, 1.0, 0.0).astype(BF16) for hit in hits], axis=1)
    acc = h_ref[...] + _dot(onehot, buf[step % 2])
    if final:
        acc = _rms(acc, nw_ref[...])
    o_ref[...] = acc


def _combine(pst_flat, h, sp, y, nw, n_exp, cap, final):
    b, s, d = h.shape
    tm = LANES
    window = tm + BF16_ROWS
    assert (n_exp * window) % LANES == 0
    kern = functools.partial(_combine_kernel, n_exp=n_exp, cap=cap, window=window, final=final)
    tok = lambda w: pl.BlockSpec((None, tm, w), lambda bb, i, pst: (bb, i, 0))
    grid_spec = pltpu.PrefetchScalarGridSpec(
        num_scalar_prefetch=1,
        grid=(b, s // tm),
        in_specs=[tok(d), tok(LANES), pl.BlockSpec(memory_space=pl.ANY),
                  pl.BlockSpec((1, d), lambda bb, i, pst: (0, 0))],
        out_specs=tok(d),
        scratch_shapes=[pltpu.VMEM((2, n_exp * window, d), BF16), pltpu.SemaphoreType.DMA((2, n_exp))],
    )
    return pl.pallas_call(
        kern,
        grid_spec=grid_spec,
        out_shape=jax.ShapeDtypeStruct((b, s, d), F32),
        compiler_params=_params(2),
        name="moe_combine",
    )(pst_flat, h, sp, y, nw)


def _lane_row(vals):
    return jnp.zeros((1, LANES), F32).at[0, :vals.shape[0]].set(vals.astype(F32))


def kernel(x, norm_mix, norm_ffn, norm_final, w_in, conv_w, gdn_a_log, gdn_dt_bias, gdn_norm, hgrn_lower_bounds,
           hgrn_norm, w_out, w_router, w_gate, w_up, w_down):
    b, s, d = x.shape
    depth = w_in.shape[0]
    ha = gdn_a_log.shape[-1]
    hd = gdn_norm.shape[-1]
    gw = ha * hd
    hw = hgrn_lower_bounds.shape[-1]
    hb = hw // hd
    n_exp = w_router.shape[-1]
    cap = CAPACITY_FACTOR * s // n_exp
    assert hd == LANES and gw == hw and s % LANES == 0 and 4 * ha <= LANES and n_exp <= LANES
    assert cap >= LANES + BF16_ROWS and cap % LANES == 0
    t = b * s
    sizes = (gw, gw, gw, gw, 2 * ha, 2 * ha, hw, 2 * hw, hw, hw)
    offs = [0]
    for sz in sizes:
        offs.append(offs[-1] + sz)
    lb_dir_major = jnp.transpose(hgrn_lower_bounds, (1, 0, 2))
    h = x.reshape(t, d)
    out = None
    for l in range(depth):
        wi = w_in[l].astype(BF16)
        part = lambda n: wi[:, offs[n]:offs[n + 1]]
        w_main = jnp.concatenate([part(0), part(1), part(2), part(3), part(6), part(7), part(8), part(9)], axis=1)
        w_small = jnp.zeros((d, LANES), BF16).at[:, :4 * ha].set(jnp.concatenate([part(4), part(5)], axis=1))
        proj, gs = _in_projection(h, norm_mix[l][None, :], w_main, w_small)
        proj3 = proj.reshape(b, s, -1)
        gs3 = gs.reshape(b, s, LANES)
        qkv = _gdn_conv(proj3, conv_w[l], gw, hd)
        alog_row = jnp.zeros((1, LANES), F32).at[0, 2 * ha:4 * ha].set(gdn_a_log[l].reshape(-1))
        dt_row = jnp.zeros((1, LANES), F32).at[0, 2 * ha:4 * ha].set(gdn_dt_bias[l].reshape(-1))
        oa = [_gdn_scan(qkv, gs3, alog_row, dt_row, ha, hd, dd).reshape(t, gw) for dd in (0, 1)]
        ob = [_hgrn_scan(proj3, lb_dir_major, hb, hd, dd, l, 4, 5 + dd, 7).reshape(t, hw) for dd in (0, 1)]
        wr_f32 = jnp.zeros((d, LANES), F32).at[:, :n_exp].set(w_router[l])
        wr_hi = wr_f32.astype(BF16)
        wr_pad = jnp.concatenate([wr_hi, (wr_f32 - wr_hi.astype(F32)).astype(BF16)], axis=1)
        hn, u, probs = _out_projection(oa[0], oa[1], ob[0], ob[1], proj, h, gdn_norm[l][None, :],
                                       hgrn_norm[l][None, :], w_out[l].astype(BF16), norm_ffn[l][None, :],
                                       wr_pad, ha, hb, hd, 3, 8, n_exp)
        sp, incl, pst = _topk_positions(probs.reshape(b, s, LANES), cap)
        pst_flat = pst[:, :, 0, :n_exp].reshape(-1)
        idx = _slot_tokens(pst_flat, incl, cap, n_exp, _slab_rows(d))
        y = _experts(idx.reshape(-1), u, w_gate[l].astype(BF16), w_up[l].astype(BF16), w_down[l].astype(BF16),
                     b, cap)
        out = _combine(pst_flat, hn.reshape(b, s, d), sp, y, norm_final[None, :], n_exp, cap, l == depth - 1)
        h = out.reshape(t, d)
    return out
```
